```python
import math
import jax
import jax.numpy as jnp
from jax import lax
import numpy as np

D_MODEL = 1024
BATCH = 16
SEQ = 4096
DEPTH = 1

GRID_W = 64
CTX_LEN = 256
NORM_EPS = 1e-6

POOL_WINDOWS = (2, 4, 8, 16)
N_POOL_GROUPS = 4
POOL_WIDTH = D_MODEL
POOL_GROUP = POOL_WIDTH // N_POOL_GROUPS

SSD_EXPAND = 2
D_INNER = SSD_EXPAND * D_MODEL
HEAD_DIM = 64
N_HEADS = D_INNER // HEAD_DIM
D_STATE = 128
N_BC_GROUPS = 4
CONV_K = 4
CONV_LEFT = CONV_K // 2
CHUNK = 128
N_DIR = 2
SSD_NORM_GROUPS = N_BC_GROUPS
CONV_DIM = D_INNER + 2 * N_BC_GROUPS * D_STATE

N_BRANCH = 2
OFF_POOL_V = 0
OFF_POOL_Z = OFF_POOL_V + POOL_WIDTH
OFF_SSD_Z = OFF_POOL_Z + POOL_WIDTH
OFF_GATE = OFF_SSD_Z + D_INNER
OFF_XBC = OFF_GATE + N_BRANCH * D_MODEL
OFF_DT = OFF_XBC + CONV_DIM
IN_COLS = OFF_DT + N_DIR * N_HEADS

kernel_name = 'hybrid_pool_ssd_diffusion_block'


def rmsnorm(x, w):
    xf = x.astype(jnp.float32)
    y = xf * lax.rsqrt(jnp.mean(xf * xf, axis=-1, keepdims=True) + NORM_EPS)
    return (y * w.astype(jnp.float32)).astype(x.dtype)


def adaln(cond, w_ada, b_ada):
    mod = jax.nn.silu(cond) @ w_ada + b_ada
    return jnp.split(mod, 3, axis=-1)


def centred_dwconv(u, w, b):
    l = u.shape[1]
    up = jnp.pad(u, ((0, 0), (CONV_LEFT, CONV_K - 1 - CONV_LEFT), (0, 0)))
    out = up[:, 0:l] * w[0]
    for k in range(1, CONV_K):
        out = out + up[:, k:k + l] * w[k]
    return out + b


def box_mean(v, k, axis):
    n = v.shape[axis]
    lo, hi = k // 2, k - 1 - k // 2
    cs = jnp.cumsum(v.astype(jnp.float32), axis=axis)
    pad = [(0, 0)] * v.ndim
    pad[axis] = (1, 0)
    cs = jnp.pad(cs, pad)
    t = jnp.arange(n)
    i_hi = jnp.minimum(t + hi + 1, n)
    i_lo = jnp.maximum(t - lo, 0)
    s = jnp.take(cs, i_hi, axis=axis) - jnp.take(cs, i_lo, axis=axis)
    shape = [1] * v.ndim
    shape[axis] = n
    cnt = (i_hi - i_lo).astype(jnp.float32).reshape(shape)
    return (s / cnt).astype(v.dtype)


def pool_mixer(v, pool_w, pool_scale, rows):
    b, l, _ = v.shape
    diffs = []
    for gi, k in enumerate(POOL_WINDOWS):
        vg = v[..., gi * POOL_GROUP:(gi + 1) * POOL_GROUP]
        if rows is None:
            m = box_mean(vg, k, 1)
        else:
            vg2 = vg.reshape(b, rows, GRID_W, POOL_GROUP)
            m = box_mean(box_mean(vg2, k, 1), k, 2).reshape(b, l, POOL_GROUP)
        diffs.append(m - vg)
    d = jnp.stack(diffs, axis=2)
    y = jnp.einsum('blgi,gio->blgo', d, pool_w).reshape(b, l, POOL_WIDTH)
    return y * pool_scale


def ssd_prep(xbc_raw, dt_raw, conv_w, conv_b, dt_bias, a_log):
    b, l, _ = xbc_raw.shape
    xbc = jax.nn.silu(centred_dwconv(xbc_raw, conv_w, conv_b))
    bc = N_BC_GROUPS * D_STATE
    xs = xbc[..., :D_INNER].reshape(b, l, N_HEADS, HEAD_DIM)
    Bm = xbc[..., D_INNER:D_INNER + bc].reshape(b, l, N_BC_GROUPS, D_STATE)
    Cm = xbc[..., D_INNER + bc:].reshape(b, l, N_BC_GROUPS, D_STATE)
    dt = jax.nn.softplus((dt_raw.reshape(b, l, N_DIR, N_HEADS) + dt_bias).astype(jnp.float32))
    A = -jnp.exp(a_log.astype(jnp.float32))
    return xs, Bm, Cm, dt, A


def ssd_scan(xs, dt, A, Bm, Cm, h0, with_output):
    b, l, h, p = xs.shape
    g, n = Bm.shape[2], Bm.shape[3]
    r = h // g
    nc = l // CHUNK
    a_cs = jnp.cumsum((dt * A).reshape(b, nc, CHUNK, g, r), axis=2)
    xdt = (xs * dt[..., None].astype(xs.dtype)).reshape(b, nc, CHUNK, g, r, p)
    Br = Bm.reshape(b, nc, CHUNK, g, n)
    a_last = a_cs[:, :, -1]
    to_end = jnp.exp(a_last[:, :, None] - a_cs).astype(xs.dtype)
    states = jnp.einsum('bcjgn,bcjgrp->bcgrpn', Br, xdt * to_end[..., None])
    chunk_decay = jnp.exp(a_last).astype(xs.dtype)
    if h0 is None:
        h0 = jnp.zeros((b, h, p, n), xs.dtype)

    def step(carry, inp):
        s, d = inp
        nxt = d[..., None, None] * carry + s
        return nxt, (carry if with_output else None)

    h_final, h_start = lax.scan(step, h0.reshape(b, g, r, p, n),
                                (jnp.moveaxis(states, 1, 0), jnp.moveaxis(chunk_decay, 1, 0)))
    h_final = h_final.reshape(b, h, p, n)
    if not with_output:
        return None, h_final
    h_start = jnp.moveaxis(h_start, 0, 1)
    Cr = Cm.reshape(b, nc, CHUNK, g, n)
    seg = a_cs[:, :, :, None] - a_cs[:, :, None]
    lower = jnp.tril(jnp.ones((CHUNK, CHUNK), dtype=bool))[:, :, None, None]
    L = jnp.exp(jnp.where(lower, seg, -jnp.inf)).astype(xs.dtype)
    cb = jnp.einsum('bcign,bcjgn->bcijg', Cr, Br)
    y_diag = jnp.einsum('bcijgr,bcjgrp->bcigrp', cb[..., None] * L, xdt)
    y_off = jnp.einsum('bcign,bcgrpn->bcigrp', Cr, h_start) * jnp.exp(a_cs).astype(xs.dtype)[..., None]
    return (y_diag + y_off).reshape(b, l, h, p), h_final


def bidir_ssd(xs, dt, A, Bm, Cm, h0_f, h0_b, with_output):
    fl = lambda t: jnp.flip(t, axis=1)
    y_f, st_f = ssd_scan(xs, dt[:, :, 0], A[0], Bm, Cm, h0_f, with_output)
    y_b, st_b = ssd_scan(fl(xs), fl(dt[:, :, 1]), A[1], fl(Bm), fl(Cm), h0_b, with_output)
    y = (y_f + fl(y_b)) if with_output else None
    return y, st_f, st_b


def gated_group_rmsnorm(y, z, w):
    b, l, d = y.shape
    u = (y * jax.nn.silu(z)).astype(jnp.float32).reshape(b, l, SSD_NORM_GROUPS, d // SSD_NORM_GROUPS)
    u = u * lax.rsqrt(jnp.mean(u * u, axis=-1, keepdims=True) + NORM_EPS)
    return (u.reshape(b, l, d) * w.astype(jnp.float32)).astype(y.dtype)


def mixer(h, w_in, b_merge, pool_w, pool_scale, conv_w, conv_b, dt_bias, a_log, d_skip, ssd_norm,
          w_proj_pool, w_proj_ssd, w_out, h0_f, h0_b, rows):
    b, l, _ = h.shape
    proj = h @ w_in
    v = proj[..., OFF_POOL_V:OFF_POOL_Z]
    z_pool = proj[..., OFF_POOL_Z:OFF_SSD_Z]
    z_ssd = proj[..., OFF_SSD_Z:OFF_GATE]
    gates = jax.nn.sigmoid(proj[..., OFF_GATE:OFF_XBC] + b_merge)
    xbc_raw = proj[..., OFF_XBC:OFF_DT]
    dt_raw = proj[..., OFF_DT:]
    y_pool = pool_mixer(v, pool_w, pool_scale, rows) * jax.nn.silu(z_pool)
    xs, Bm, Cm, dt, A = ssd_prep(xbc_raw, dt_raw, conv_w, conv_b, dt_bias, a_log)
    y_ssd, st_f, st_b = bidir_ssd(xs, dt, A, Bm, Cm, h0_f, h0_b, True)
    y_ssd = (y_ssd + d_skip[:, None] * xs).reshape(b, l, D_INNER)
    y_ssd = gated_group_rmsnorm(y_ssd, z_ssd, ssd_norm)
    merged = gates[..., :D_MODEL] * (y_pool @ w_proj_pool) + gates[..., D_MODEL:] * (y_ssd @ w_proj_ssd)
    return merged @ w_out, st_f, st_b


def context_states(hc, w_in, conv_w, conv_b, dt_bias, a_log):
    proj = hc @ w_in[:, OFF_XBC:]
    xs, Bm, Cm, dt, A = ssd_prep(proj[..., :CONV_DIM], proj[..., CONV_DIM:], conv_w, conv_b, dt_bias, a_log)
    _, st_f, st_b = bidir_ssd(xs, dt, A, Bm, Cm, None, None, False)
    return st_f, st_b


def _normal(key, shape, scale):
    return jax.random.normal(key, shape, jnp.float32) * scale


def setup_inputs(seed: int = 0) -> dict:
    key = jax.random.key(seed)
    ks = jax.random.split(key, 24)
    D = D_MODEL
    dt0 = jnp.exp(jax.random.uniform(ks[12], (DEPTH, N_DIR, N_HEADS), jnp.float32,
                                     minval=math.log(1e-3), maxval=math.log(1e-1)))
    return {
        'x': _normal(ks[0], (BATCH, SEQ, D), 1.0),
        'c': _normal(ks[1], (BATCH, D), 1.0),
        'ctx': _normal(ks[2], (BATCH, CTX_LEN, D), 1.0),
        'c_ctx': _normal(ks[3], (D,), 1.0),
        'w_ada': _normal(ks[4], (DEPTH, D, 3 * D), 0.5 * D ** -0.5),
        'b_ada': _normal(ks[5], (DEPTH, 3 * D), 0.02),
        'norm_pre': 1.0 + _normal(ks[6], (DEPTH, D), 0.05),
        'norm_post': 1.0 + _normal(ks[7], (DEPTH, D), 0.05),
        'w_in': _normal(ks[8], (DEPTH, D, IN_COLS), D ** -0.5),
        'b_merge': _normal(ks[9], (DEPTH, N_BRANCH * D), 0.02),
        'pool_w': _normal(ks[10], (DEPTH, N_POOL_GROUPS, POOL_GROUP, POOL_GROUP), POOL_GROUP ** -0.5),
        'pool_scale': 1.0 + _normal(ks[11], (DEPTH, POOL_WIDTH), 0.05),
        'conv_w': _normal(ks[13], (DEPTH, CONV_K, CONV_DIM), CONV_K ** -0.5),
        'conv_b': _normal(ks[14], (DEPTH, CONV_DIM), 0.02),
        'dt_bias': dt0 + jnp.log(-jnp.expm1(-dt0)),
        'a_log': jnp.log(jax.random.uniform(ks[15], (DEPTH, N_DIR, N_HEADS), jnp.float32, minval=1.0, maxval=16.0)),
        'd_skip': 1.0 + _normal(ks[16], (DEPTH, N_HEADS), 0.05),
        'ssd_norm': 1.0 + _normal(ks[17], (DEPTH, D_INNER), 0.05),
        'w_proj_pool': _normal(ks[18], (DEPTH, POOL_WIDTH, D), POOL_WIDTH ** -0.5),
        'w_proj_ssd': _normal(ks[19], (DEPTH, D_INNER, D), D_INNER ** -0.5),
        'w_out': _normal(ks[20], (DEPTH, D, D), D ** -0.5),
    }


def reference(x, c, ctx, c_ctx, w_ada, b_ada, norm_pre, norm_post, w_in, b_merge, pool_w, pool_scale,
              conv_w, conv_b, dt_bias, a_log, d_skip, ssd_norm, w_proj_pool, w_proj_ssd, w_out):
    rows = x.shape[1] // GRID_W
    for layer in range(DEPTH):
        shift, scale, gate = adaln(c[:, None, :], w_ada[layer], b_ada[layer])
        shift_c, scale_c, gate_c = adaln(c_ctx, w_ada[layer], b_ada[layer])
        hc = rmsnorm(ctx, norm_pre[layer]) * (1.0 + scale_c) + shift_c
        if layer + 1 < DEPTH:
            out_c, st_f, st_b = mixer(hc, w_in[layer], b_merge[layer], pool_w[layer], pool_scale[layer],
                                      conv_w[layer], conv_b[layer], dt_bias[layer], a_log[layer],
                                      d_skip[layer], ssd_norm[layer], w_proj_pool[layer],
                                      w_proj_ssd[layer], w_out[layer], None, None, None)
            ctx_next = ctx + gate_c * rmsnorm(out_c, norm_post[layer])
        else:
            st_f, st_b = context_states(hc, w_in[layer], conv_w[layer], conv_b[layer],
                                        dt_bias[layer], a_log[layer])
            ctx_next = ctx
        hx = rmsnorm(x, norm_pre[layer]) * (1.0 + scale) + shift
        out_x, _, _ = mixer(hx, w_in[layer], b_merge[layer], pool_w[layer], pool_scale[layer],
                            conv_w[layer], conv_b[layer], dt_bias[layer], a_log[layer],
                            d_skip[layer], ssd_norm[layer], w_proj_pool[layer],
                            w_proj_ssd[layer], w_out[layer], st_f, st_b, rows)
        x = x + gate * rmsnorm(out_x, norm_post[layer])
        ctx = ctx_next
    return x
```

```python
import functools

import jax
import jax.numpy as jnp
from jax import lax
from jax.experimental import pallas as pl
from jax.experimental.pallas import tpu as pltpu

F32 = jnp.float32
BF16 = jnp.bfloat16

D_MODEL = 1024
GRID_W = 64
NORM_EPS = 1e-6
POOL_WINDOWS = (2, 4, 8, 16)
N_POOL_GROUPS = 4
POOL_GROUP = D_MODEL // N_POOL_GROUPS
D_INNER = 2 * D_MODEL
HEAD_DIM = 64
N_HEADS = D_INNER // HEAD_DIM
D_STATE = 128
N_BC_GROUPS = 4
HEADS_PER_GROUP = N_HEADS // N_BC_GROUPS
GROUP_INNER = D_INNER // N_BC_GROUPS
CONV_K = 4
CONV_LEFT = CONV_K // 2
CHUNK = 128
N_DIR = 2
CONV_DIM = D_INNER + 2 * N_BC_GROUPS * D_STATE
OFF_POOL_V = 0
OFF_POOL_Z = OFF_POOL_V + D_MODEL
OFF_SSD_Z = OFF_POOL_Z + D_MODEL
OFF_GATE = OFF_SSD_Z + D_INNER
OFF_XBC = OFF_GATE + 2 * D_MODEL
OFF_DT = OFF_XBC + CONV_DIM
IN_COLS = OFF_DT + N_DIR * N_HEADS

LANES = 128
SUBLANES = 8
BF16_ROWS = 16
VMEM_LIMIT_BYTES = 60 * 1024 * 1024

DT_COLS = N_BC_GROUPS * LANES

TM_PROJ = 512
TN_PROJ = 512
TM_OUT = 512
CONV_HALO = BF16_ROWS


def _sigmoid(x):
    return 1.0 / (1.0 + jnp.exp(-x))


def _silu(x):
    return x * _sigmoid(x)


def _softplus(x):
    return jnp.maximum(x, 0.0) + jnp.log1p(jnp.exp(-jnp.abs(x)))


def _split3(a):
    hi = a.astype(BF16)
    r1 = a - hi.astype(F32)
    mid = r1.astype(BF16)
    lo = (r1 - mid.astype(F32)).astype(BF16)
    return hi, mid, lo


def _dot(a, b):
    return jnp.dot(a, b, preferred_element_type=F32)


def _dot3(a_f32, b_bf16):
    hi, mid, lo = _split3(a_f32)
    return _dot(hi, b_bf16) + _dot(mid, b_bf16) + _dot(lo, b_bf16)


def _mod_kernel(c_ref, w_ref, b_ref, o_ref):
    s = _silu(c_ref[...])
    o_ref[...] = _dot(s, w_ref[...]) + b_ref[...]


def _mod_call(cc, w_ada, b_ada):
    rows = cc.shape[0]
    tn = 512
    return pl.pallas_call(
        _mod_kernel,
        grid=(3 * D_MODEL // tn,),
        in_specs=[
            pl.BlockSpec((rows, D_MODEL), lambda j: (0, 0)),
            pl.BlockSpec((D_MODEL, tn), lambda j: (0, j)),
            pl.BlockSpec((1, tn), lambda j: (0, j)),
        ],
        out_specs=pl.BlockSpec((rows, tn), lambda j: (0, j)),
        out_shape=jax.ShapeDtypeStruct((rows, 3 * D_MODEL), F32),
        name="mod",
    )(cc, w_ada, b_ada)


def _norm_modulate(x, wpre, scale, shift):
    ms = jnp.mean(x * x, axis=-1, keepdims=True)
    xn = x * lax.rsqrt(ms + NORM_EPS)
    return (xn * wpre) * (1.0 + scale) + shift


def _dt_epilogue(acc, dtb, alog, dt_ref, acs_ref, tm):
    lane = lax.broadcasted_iota(jnp.int32, (1, DT_COLS), 1) & (LANES - 1)
    used = lane < 2 * HEADS_PER_GROUP
    is_bwd = used & (lane >= HEADS_PER_GROUP)
    dt = _softplus(acc + dtb)
    a = dt * jnp.where(used, -jnp.exp(alog), 0.0)
    dt_ref[0] = dt
    ri = lax.broadcasted_iota(jnp.int32, (CHUNK, CHUNK), 0)
    ci = lax.broadcasted_iota(jnp.int32, (CHUNK, CHUNK), 1)
    tri = (ri >= ci).astype(BF16)
    for q in range(tm // CHUNK):
        aq = a[q * CHUNK:(q + 1) * CHUNK]
        fwd = _dot3_lhs01(tri, aq)
        tot = fwd[CHUNK - 1:CHUNK]
        bwd = tot - fwd + aq
        acs_ref[0, q * CHUNK:(q + 1) * CHUNK, :] = jnp.where(is_bwd, bwd, fwd)


def _dot3_lhs01(m01_bf16, a_f32):
    hi, mid, lo = _split3(a_f32)
    return _dot(m01_bf16, hi) + _dot(m01_bf16, mid) + _dot(m01_bf16, lo)


def _inproj_main_kernel(x_ref, sh_ref, sc_ref, wpre_ref, w_ref, wdt_ref, bm_ref, dtb_ref, alog_ref,
                        v_ref, zp_ref, zs_ref, gt_ref, xbc_ref, dt_ref, acs_ref, *, tm):
    h = _norm_modulate(x_ref[0], wpre_ref[...], sc_ref[0], sh_ref[0])
    hb = h.astype(BF16)
    for s in range(OFF_DT // TN_PROJ):
        col = s * TN_PROJ
        acc = _dot(hb, w_ref[:, col:col + TN_PROJ])
        if col < OFF_POOL_Z:
            v_ref[0, :, col - OFF_POOL_V:col - OFF_POOL_V + TN_PROJ] = acc.astype(BF16)
        elif col < OFF_SSD_Z:
            zp_ref[0, :, col - OFF_POOL_Z:col - OFF_POOL_Z + TN_PROJ] = _silu(acc).astype(BF16)
        elif col < OFF_GATE:
            zs_ref[0, :, col - OFF_SSD_Z:col - OFF_SSD_Z + TN_PROJ] = _silu(acc).astype(BF16)
        elif col < OFF_XBC:
            o = col - OFF_GATE
            gt_ref[0, :, o:o + TN_PROJ] = _sigmoid(acc + bm_ref[:, o:o + TN_PROJ]).astype(BF16)
        else:
            o = col - OFF_XBC
            xbc_ref[0, :, o:o + TN_PROJ] = acc.astype(BF16)
    acc = _dot(hb, wdt_ref[...])
    _dt_epilogue(acc, dtb_ref[...], alog_ref[...], dt_ref, acs_ref, tm)


def _inproj_ctx_kernel(x_ref, sh_ref, sc_ref, wpre_ref, w_ref, wdt_ref, dtb_ref, alog_ref,
                       xbc_ref, dt_ref, acs_ref, *, tm):
    h = _norm_modulate(x_ref[0], wpre_ref[...], sc_ref[0], sh_ref[0])
    hb = h.astype(BF16)
    for s in range(CONV_DIM // TN_PROJ):
        col = s * TN_PROJ
        acc = _dot(hb, w_ref[:, OFF_XBC + col:OFF_XBC + col + TN_PROJ])
        xbc_ref[0, :, col:col + TN_PROJ] = acc.astype(BF16)
    acc = _dot(hb, wdt_ref[...])
    _dt_epilogue(acc, dtb_ref[...], alog_ref[...], dt_ref, acs_ref, tm)


def _resident(shape):
    nd = len(shape)
    return pl.BlockSpec(shape, lambda *_: (0,) * nd, pipeline_mode=pl.Buffered(1))


def _inproj_call(xin, mod3, mod_row, wpre, w_main, w_dt, b_merge, dtb_lay, alog_lay, *, ctx):
    bn, ln, _ = xin.shape
    tm = min(TM_PROJ, ln)
    grid = (bn, ln // tm)
    if mod_row is None:
        row = lambda b, i: b
    else:
        row = lambda b, i: mod_row
    tok = lambda width: pl.BlockSpec((1, tm, width), lambda b, i: (b, i, 0))
    in_specs = [
        tok(D_MODEL),
        pl.BlockSpec((1, 1, D_MODEL), lambda b, i: (row(b, i), 0, 0)),
        pl.BlockSpec((1, 1, D_MODEL), lambda b, i: (row(b, i), 0, 1)),
        _resident((1, D_MODEL)),
        _resident(w_main.shape),
        _resident(w_dt.shape),
    ]
    args = [xin, mod3, mod3, wpre, w_main, w_dt]
    if not ctx:
        in_specs.append(_resident(b_merge.shape))
        args.append(b_merge)
    in_specs += [_resident(dtb_lay.shape), _resident(alog_lay.shape)]
    args += [dtb_lay, alog_lay]
    tail_specs = [tok(CONV_DIM), tok(DT_COLS), tok(DT_COLS)]
    tail_shapes = [
        jax.ShapeDtypeStruct((bn, ln, CONV_DIM), BF16),
        jax.ShapeDtypeStruct((bn, ln, DT_COLS), F32),
        jax.ShapeDtypeStruct((bn, ln, DT_COLS), F32),
    ]
    if ctx:
        body = functools.partial(_inproj_ctx_kernel, tm=tm)
        out_specs, out_shape = tail_specs, tail_shapes
    else:
        body = functools.partial(_inproj_main_kernel, tm=tm)
        out_specs = [tok(D_MODEL), tok(D_MODEL), tok(D_INNER), tok(2 * D_MODEL)] + tail_specs
        out_shape = [
            jax.ShapeDtypeStruct((bn, ln, D_MODEL), BF16),
            jax.ShapeDtypeStruct((bn, ln, D_MODEL), BF16),
            jax.ShapeDtypeStruct((bn, ln, D_INNER), BF16),
            jax.ShapeDtypeStruct((bn, ln, 2 * D_MODEL), BF16),
        ] + tail_shapes
    return pl.pallas_call(
        body,
        grid=grid,
        in_specs=in_specs,
        out_specs=out_specs,
        out_shape=out_shape,
        compiler_params=pltpu.CompilerParams(
            dimension_semantics=("parallel", "parallel"), vmem_limit_bytes=VMEM_LIMIT_BYTES),
        name="inproj_ctx" if ctx else "inproj",
    )(*args)


def _ssd_kernel(*refs, nc, with_output, has_h0):
    it = iter(refs)
    x_ref, b_ref, c_ref, dt_ref, acs_ref = (next(it) for _ in range(5))
    cwx_ref, cwb_ref, cwc_ref, cbx_ref, cbb_ref, cbc_ref = (next(it) for _ in range(6))
    h0_ref = next(it) if has_h0 else None
    if with_output:
        z_ref, dsk_ref, nw_ref = (next(it) for _ in range(3))
        y_ref = next(it)
    else:
        st_ref = next(it)
    xs_s, bs_s, cs_s, cv_s, ef_s, eb_s, hf_s, hb_s = (next(it) for _ in range(8))
    hbin_s = next(it) if with_output else None

    ls = nc * CHUNK
    gi = GROUP_INNER
    hpg = HEADS_PER_GROUP

    er = lax.broadcasted_iota(jnp.int32, (LANES, gi), 0)
    el = lax.broadcasted_iota(jnp.int32, (LANES, gi), 1) // HEAD_DIM
    ef_s[...] = (er == el).astype(BF16)
    eb_s[...] = (er == el + hpg).astype(BF16)

    cols = ((x_ref, xs_s, cwx_ref, cbx_ref, 0, gi),
            (b_ref, bs_s, cwb_ref, cbb_ref, gi, D_STATE),
            (c_ref, cs_s, cwc_ref, cbc_ref, gi + D_STATE, D_STATE))

    def conv_chunk(c, carry):
        r0 = pl.multiple_of(c * CHUNK, CHUNK)
        rp = pl.multiple_of(jnp.maximum(r0 - CONV_HALO, 0), CONV_HALO)
        rn = pl.multiple_of(jnp.minimum(r0 + CHUNK, ls - CONV_HALO), CONV_HALO)
        mp = (c > 0).astype(F32)
        mn = (c < nc - 1).astype(F32)
        for src, dst, cw, cb, off, wd in cols:
            cv_s[0:CONV_HALO, off:off + wd] = src[0, pl.ds(rp, CONV_HALO), :].astype(F32) * mp
            cv_s[CONV_HALO:CONV_HALO + CHUNK, off:off + wd] = src[0, pl.ds(r0, CHUNK), :].astype(F32)
            cv_s[CONV_HALO + CHUNK:2 * CONV_HALO + CHUNK, off:off + wd] = (
                src[0, pl.ds(rn, CONV_HALO), :].astype(F32) * mn)
        for src, dst, cw, cb, off, wd in cols:
            base = CONV_HALO - CONV_LEFT
            acc = cv_s[base:base + CHUNK, off:off + wd] * cw[0:1, :]
            for k in range(1, CONV_K):
                acc = acc + cv_s[base + k:base + k + CHUNK, off:off + wd] * cw[k:k + 1, :]
            acc = acc + cb[...]
            dst[pl.ds(r0, CHUNK), :] = _silu(acc).astype(BF16)
        return carry

    lax.fori_loop(0, nc, conv_chunk, 0)

    lane = lax.broadcasted_iota(jnp.int32, (1, LANES), 1)
    is_bwd = (lane >= hpg) & (lane < 2 * hpg)

    def chunk_state_terms(c):
        r0 = pl.multiple_of(c * CHUNK, CHUNK)
        dtc = dt_ref[0, pl.ds(r0, CHUNK), :]
        acs = acs_ref[0, pl.ds(r0, CHUNK), :]
        tot = jnp.where(is_bwd, acs[0:1, :], acs[CHUNK - 1:CHUNK, :])
        w = dtc * jnp.exp(tot - acs)
        return r0, dtc, acs, tot, w

    def local_state(r0, w, e_s):
        xw = xs_s[pl.ds(r0, CHUNK), :].astype(F32) * _dot(w.astype(BF16), e_s[...])
        bc = bs_s[pl.ds(r0, CHUNK), :]
        return lax.dot_general(bc, xw.astype(BF16), (((0,), (0,)), ((), ())),
                               preferred_element_type=F32)

    def chunk_decay(tot, e_s):
        t8 = jnp.broadcast_to(tot, (SUBLANES, LANES))
        return jnp.exp(_dot3(t8, e_s[...])[0:1, :])

    if has_h0:
        hf_s[...] = h0_ref[0, 0, 0]
        hb_s[...] = h0_ref[0, 0, 1]
    else:
        hf_s[...] = jnp.zeros_like(hf_s)
        hb_s[...] = jnp.zeros_like(hb_s)

    def bwd_chunk(i, carry):
        c = nc - 1 - i
        r0, dtc, acs, tot, w = chunk_state_terms(c)
        if with_output:
            hbin_s[c] = hb_s[...].astype(BF16)
        s_loc = local_state(r0, w, eb_s)
        hb_s[...] = hb_s[...] * chunk_decay(tot, eb_s) + s_loc
        return carry

    lax.fori_loop(0, nc, bwd_chunk, 0)

    ri = lax.broadcasted_iota(jnp.int32, (CHUNK, CHUNK), 0)
    ci = lax.broadcasted_iota(jnp.int32, (CHUNK, CHUNK), 1)
    lower = ri > ci
    upper = ri < ci
    pair_lane = lax.broadcasted_iota(jnp.int32, (CHUNK, 2 * HEAD_DIM), 1)
    first_head = pair_lane < HEAD_DIM

    def fwd_chunk(c, carry):
        r0, dtc, acs, tot, w = chunk_state_terms(c)
        if with_output:
            xc = xs_s[pl.ds(r0, CHUNK), :]
            bc = bs_s[pl.ds(r0, CHUNK), :]
            cc = cs_s[pl.ds(r0, CHUNK), :]
            cb = lax.dot_general(cc, bc, (((1,), (1,)), ((), ())), preferred_element_type=F32)
            acs_t = acs.T
            dt_t = dtc.T
            pieces = []
            for p in range(hpg // 2):
                ms = []
                for r in (2 * p, 2 * p + 1):
                    colf = acs[:, r:r + 1]
                    colb = acs[:, hpg + r:hpg + r + 1]
                    rowf = acs_t[r:r + 1, :]
                    rowb = acs_t[hpg + r:hpg + r + 1, :]
                    dtf = dt_t[r:r + 1, :]
                    dtb = dt_t[hpg + r:hpg + r + 1, :]
                    arg = jnp.where(lower, colf, colb) - jnp.where(lower, rowf, rowb)
                    wrow = jnp.where(lower, dtf, jnp.where(upper, dtb, dtf + dtb))
                    ms.append((cb * jnp.exp(arg) * wrow).astype(BF16))
                xp = xc[:, p * 2 * HEAD_DIM:(p + 1) * 2 * HEAD_DIM]
                zero = jnp.zeros_like(xp)
                rhs = jnp.concatenate([jnp.where(first_head, xp, zero),
                                       jnp.where(first_head, zero, xp)], axis=0)
                pieces.append(_dot(jnp.concatenate(ms, axis=1), rhs))
            y = jnp.concatenate(pieces, axis=1)
            eacs = jnp.exp(acs).astype(BF16)
            y = y + _dot(cc, hf_s[...].astype(BF16)) * _dot(eacs, ef_s[...])
            y = y + _dot(cc, hbin_s[c]) * _dot(eacs, eb_s[...])
            y = y + dsk_ref[...] * xc.astype(F32)
            u = y * z_ref[0, pl.ds(r0, CHUNK), :].astype(F32)
            ms_u = jnp.mean(u * u, axis=-1, keepdims=True)
            y_ref[0, pl.ds(r0, CHUNK), :] = (
                (u * lax.rsqrt(ms_u + NORM_EPS)) * nw_ref[...]).astype(BF16)
        s_loc = local_state(r0, w, ef_s)
        hf_s[...] = hf_s[...] * chunk_decay(tot, ef_s) + s_loc
        return carry

    lax.fori_loop(0, nc, fwd_chunk, 0)

    if not with_output:
        st_ref[0, 0, 0] = hf_s[...]
        st_ref[0, 0, 1] = hb_s[...]


def _ssd_call(xbc, dt, acs, conv_w, conv_b, h0, z, dsk, nw):
    bn, ls, _ = xbc.shape
    nc = ls // CHUNK
    with_output = z is not None
    has_h0 = h0 is not None
    gi = GROUP_INNER
    xb = gi // D_STATE
    n_x = D_INNER // D_STATE
    in_specs = [
        pl.BlockSpec((1, ls, gi), lambda b, g: (b, 0, g)),
        pl.BlockSpec((1, ls, D_STATE), lambda b, g: (b, 0, n_x + g)),
        pl.BlockSpec((1, ls, D_STATE), lambda b, g: (b, 0, n_x + N_BC_GROUPS + g)),
        pl.BlockSpec((1, ls, LANES), lambda b, g: (b, 0, g)),
        pl.BlockSpec((1, ls, LANES), lambda b, g: (b, 0, g)),
        pl.BlockSpec((CONV_K, gi), lambda b, g: (0, g)),
        pl.BlockSpec((CONV_K, D_STATE), lambda b, g: (0, n_x + g)),
        pl.BlockSpec((CONV_K, D_STATE), lambda b, g: (0, n_x + N_BC_GROUPS + g)),
        pl.BlockSpec((1, gi), lambda b, g: (0, g)),
        pl.BlockSpec((1, D_STATE), lambda b, g: (0, n_x + g)),
        pl.BlockSpec((1, D_STATE), lambda b, g: (0, n_x + N_BC_GROUPS + g)),
    ]
    del xb
    args = [xbc, xbc, xbc, dt, acs, conv_w, conv_w, conv_w, conv_b, conv_b, conv_b]
    st_spec = pl.BlockSpec((1, 1, N_DIR, D_STATE, gi), lambda b, g: (b, g, 0, 0, 0))
    if has_h0:
        in_specs.append(st_spec)
        args.append(h0)
    if with_output:
        in_specs += [
            pl.BlockSpec((1, ls, gi), lambda b, g: (b, 0, g)),
            pl.BlockSpec((1, gi), lambda b, g: (0, g)),
            pl.BlockSpec((1, gi), lambda b, g: (0, g)),
        ]
        args += [z, dsk, nw]
        out_specs = pl.BlockSpec((1, ls, gi), lambda b, g: (b, 0, g))
        out_shape = jax.ShapeDtypeStruct((bn, ls, D_INNER), BF16)
    else:
        out_specs = st_spec
        out_shape = jax.ShapeDtypeStruct((bn, N_BC_GROUPS, N_DIR, D_STATE, gi), F32)
    scratch = [
        pltpu.VMEM((ls, gi), BF16),
        pltpu.VMEM((ls, D_STATE), BF16),
        pltpu.VMEM((ls, D_STATE), BF16),
        pltpu.VMEM((CHUNK + 2 * CONV_HALO, gi + 2 * D_STATE), F32),
        pltpu.VMEM((LANES, gi), BF16),
        pltpu.VMEM((LANES, gi), BF16),
        pltpu.VMEM((D_STATE, gi), F32),
        pltpu.VMEM((D_STATE, gi), F32),
    ]
    if with_output:
        scratch.append(pltpu.VMEM((nc, D_STATE, gi), BF16))
    return pl.pallas_call(
        functools.partial(_ssd_kernel, nc=nc, with_output=with_output, has_h0=has_h0),
        grid=(bn, N_BC_GROUPS),
        in_specs=in_specs,
        out_specs=out_specs,
        out_shape=out_shape,
        scratch_shapes=scratch,
        compiler_params=pltpu.CompilerParams(
            dimension_semantics=("parallel", "parallel"), vmem_limit_bytes=VMEM_LIMIT_BYTES),
        name="ssd" if with_output else "ssd_ctx",
    )(*args)


POOL_TB = 4 * GRID_W
POOL_PAD_ROWS = 8
POOL_MM_ROWS = 512


def _pool_kernel(v_ref, zp_ref, pw_ref, ps_ref, o_ref, cm_s, d_s, csc_s, *, ls):
    g = pl.program_id(1)
    rows = ls // GRID_W
    pad = POOL_PAD_ROWS * GRID_W
    pg = POOL_GROUP

    def body(k):
        lo, hi = k // 2, k - 1 - k // 2
        ti = lax.broadcasted_iota(jnp.int32, (POOL_TB, POOL_TB), 0)
        si = lax.broadcasted_iota(jnp.int32, (POOL_TB, POOL_TB), 1)
        dlt = si - ti
        band = ((ti // GRID_W) == (si // GRID_W)) & (dlt >= -lo) & (dlt <= hi)
        pmat = band.astype(BF16)
        tc = lax.broadcasted_iota(jnp.int32, (POOL_TB, pg), 0) % GRID_W
        cnt_c = jnp.minimum(tc + hi, GRID_W - 1) - jnp.maximum(tc - lo, 0) + 1
        csc_s[...] = 1.0 / cnt_c.astype(F32)
        cm_s[0:pad, :] = jnp.zeros((pad, pg), F32)
        cm_s[pad + ls:2 * pad + ls, :] = jnp.zeros((pad, pg), F32)

        def col_pass(tb, carry):
            r0 = pl.multiple_of(tb * POOL_TB, POOL_TB)
            cm = _dot(pmat, v_ref[0, pl.ds(r0, POOL_TB), :]) * csc_s[...]
            cm_s[pl.ds(pad + r0, POOL_TB), :] = cm
            return carry

        lax.fori_loop(0, ls // POOL_TB, col_pass, 0)

        def row_pass(r, carry):
            r0 = pl.multiple_of(r * GRID_W, GRID_W)
            acc = cm_s[pl.ds(pad + r0 - lo * GRID_W, GRID_W), :]
            for o in range(-lo + 1, hi + 1):
                acc = acc + cm_s[pl.ds(pad + r0 + o * GRID_W, GRID_W), :]
            cnt_r = jnp.minimum(r + hi, rows - 1) - jnp.maximum(r - lo, 0) + 1
            inv_r = 1.0 / jnp.full((1, pg), cnt_r, jnp.int32).astype(F32)
            m = acc * inv_r
            d = m - v_ref[0, pl.ds(r0, GRID_W), :].astype(F32)
            d_s[pl.ds(r0, GRID_W), :] = d.astype(BF16)
            return carry

        lax.fori_loop(0, rows, row_pass, 0)

        def mm(tb, carry):
            r0 = pl.multiple_of(tb * POOL_MM_ROWS, POOL_MM_ROWS)
            y = _dot(d_s[pl.ds(r0, POOL_MM_ROWS), :], pw_ref[0])
            y = (y * ps_ref[...]) * zp_ref[0, pl.ds(r0, POOL_MM_ROWS), :].astype(F32)
            o_ref[0, pl.ds(r0, POOL_MM_ROWS), :] = y.astype(BF16)
            return carry

        lax.fori_loop(0, ls // POOL_MM_ROWS, mm, 0)

    for gidx, k in enumerate(POOL_WINDOWS):
        pl.when(g == gidx)(functools.partial(body, k))


def _pool_call(v, zp, pool_w, pool_scale):
    bn, ls, _ = v.shape
    pg = POOL_GROUP
    tok = pl.BlockSpec((1, ls, pg), lambda b, g: (b, 0, g))
    return pl.pallas_call(
        functools.partial(_pool_kernel, ls=ls),
        grid=(bn, N_POOL_GROUPS),
        in_specs=[
            tok,
            tok,
            pl.BlockSpec((1, pg, pg), lambda b, g: (g, 0, 0)),
            pl.BlockSpec((1, pg), lambda b, g: (0, g)),
        ],
        out_specs=tok,
        out_shape=jax.ShapeDtypeStruct((bn, ls, D_MODEL), BF16),
        scratch_shapes=[
            pltpu.VMEM((ls + 2 * POOL_PAD_ROWS * GRID_W, pg), F32),
            pltpu.VMEM((ls, pg), BF16),
            pltpu.VMEM((POOL_TB, pg), F32),
        ],
        compiler_params=pltpu.CompilerParams(
            dimension_semantics=("parallel", "parallel"), vmem_limit_bytes=VMEM_LIMIT_BYTES),
        name="pool",
    )(v, zp, pool_w, pool_scale)


def _out_kernel(yp_ref, ys_ref, gt_ref, x_ref, gate_ref, wpost_ref, wpp_ref, wps_ref, wo_ref, o_ref):
    a = _dot(yp_ref[0], wpp_ref[...])
    b = _dot(ys_ref[0], wps_ref[...])
    merged = (gt_ref[0, :, 0:D_MODEL].astype(F32) * a
              + gt_ref[0, :, D_MODEL:2 * D_MODEL].astype(F32) * b)
    o = _dot(merged.astype(BF16), wo_ref[...])
    ms = jnp.mean(o * o, axis=-1, keepdims=True)
    on = (o * lax.rsqrt(ms + NORM_EPS)) * wpost_ref[...]
    o_ref[0] = x_ref[0] + gate_ref[0] * on


def _out_call(yp, ys, gates, x, mod3, wpost, wpp, wps, wo):
    bn, ls, _ = x.shape
    tm = TM_OUT
    tok = lambda width: pl.BlockSpec((1, tm, width), lambda b, i: (b, i, 0))
    return pl.pallas_call(
        _out_kernel,
        grid=(bn, ls // tm),
        in_specs=[
            tok(D_MODEL), tok(D_INNER), tok(2 * D_MODEL), tok(D_MODEL),
            pl.BlockSpec((1, 1, D_MODEL), lambda b, i: (b, 0, 2)),
            _resident(wpost.shape), _resident(wpp.shape), _resident(wps.shape), _resident(wo.shape),
        ],
        out_specs=tok(D_MODEL),
        out_shape=jax.ShapeDtypeStruct(x.shape, F32),
        compiler_params=pltpu.CompilerParams(
            dimension_semantics=("parallel", "parallel"), vmem_limit_bytes=VMEM_LIMIT_BYTES),
        name="out",
    )(yp, ys, gates, x, mod3, wpost, wpp, wps, wo)


def _dt_layout_index():
    idx = []
    zero_col = N_DIR * N_HEADS
    for g in range(N_BC_GROUPS):
        blk = [g * HEADS_PER_GROUP + r for r in range(HEADS_PER_GROUP)]
        blk += [N_HEADS + g * HEADS_PER_GROUP + r for r in range(HEADS_PER_GROUP)]
        blk += [zero_col] * (LANES - len(blk))
        idx += blk
    return jnp.asarray(idx, jnp.int32)


def _dt_layout(cols):
    padded = jnp.concatenate([cols, jnp.zeros((cols.shape[0], 1), cols.dtype)], axis=1)
    return jnp.take(padded, _dt_layout_index(), axis=1)


def kernel(x, c, ctx, c_ctx, w_ada, b_ada, norm_pre, norm_post, w_in, b_merge, pool_w, pool_scale,
           conv_w, conv_b, dt_bias, a_log, d_skip, ssd_norm, w_proj_pool, w_proj_ssd, w_out):
    assert w_ada.shape[0] == 1, "single-layer block"
    bn = x.shape[0]
    assert x.shape[1] % (GRID_W * GRID_W) == 0 and x.shape[1] // GRID_W == GRID_W

    n_cond = bn + 1
    cond_rows = -(-n_cond // SUBLANES) * SUBLANES
    cc = jnp.concatenate([c, c_ctx[None, :], jnp.zeros((cond_rows - n_cond, D_MODEL), F32)], axis=0)
    mod = _mod_call(cc, w_ada[0], b_ada[0][None, :])
    mod3 = mod.reshape(cond_rows, 1, 3 * D_MODEL)

    w_in0 = w_in[0]
    w_main = w_in0[:, :OFF_DT].astype(BF16)
    w_dt = _dt_layout(w_in0[:, OFF_DT:]).astype(BF16)
    dtb_lay = _dt_layout(dt_bias[0].reshape(1, N_DIR * N_HEADS))
    alog_lay = _dt_layout(a_log[0].reshape(1, N_DIR * N_HEADS))
    wpre = norm_pre[0][None, :]
    wpost = norm_post[0][None, :]
    cw = conv_w[0]
    cb = conv_b[0][None, :]
    dsk = jnp.repeat(d_skip[0], HEAD_DIM)[None, :]
    nw = ssd_norm[0][None, :]

    xbc_c, dt_c, acs_c = _inproj_call(ctx, mod3, bn, wpre, w_main, w_dt, None, dtb_lay, alog_lay,
                                      ctx=True)
    h0 = _ssd_call(xbc_c, dt_c, acs_c, cw, cb, None, None, None, None)

    v, zp, zs, gates, xbc, dt, acs = _inproj_call(
        x, mod3, None, wpre, w_main, w_dt, b_merge[0][None, :], dtb_lay, alog_lay, ctx=False)
    y_pool = _pool_call(v, zp, pool_w[0].astype(BF16), pool_scale[0][None, :])
    y_ssd = _ssd_call(xbc, dt, acs, cw, cb, h0, zs, dsk, nw)
    return _out_call(y_pool, y_ssd, gates, x, mod3, wpost,
                     w_proj_pool[0].astype(BF16), w_proj_ssd[0].astype(BF16), w_out[0].astype(BF16))
```

```python
import functools

import jax
import jax.numpy as jnp
from jax import lax
from jax.experimental import pallas as pl
from jax.experimental.pallas import tpu as pltpu

F32 = jnp.float32
BF16 = jnp.bfloat16

D_MODEL = 1024
GRID_W = 64
NORM_EPS = 1e-6
POOL_WINDOWS = (2, 4, 8, 16)
N_POOL_GROUPS = 4
POOL_GROUP = D_MODEL // N_POOL_GROUPS
D_INNER = 2 * D_MODEL
HEAD_DIM = 64
N_HEADS = D_INNER // HEAD_DIM
D_STATE = 128
N_BC_GROUPS = 4
HEADS_PER_GROUP = N_HEADS // N_BC_GROUPS
GROUP_INNER = D_INNER // N_BC_GROUPS
CONV_K = 4
CONV_LEFT = CONV_K // 2
CHUNK = 128
N_DIR = 2
CONV_DIM = D_INNER + 2 * N_BC_GROUPS * D_STATE
OFF_POOL_V = 0
OFF_POOL_Z = OFF_POOL_V + D_MODEL
OFF_SSD_Z = OFF_POOL_Z + D_MODEL
OFF_GATE = OFF_SSD_Z + D_INNER
OFF_XBC = OFF_GATE + 2 * D_MODEL
OFF_DT = OFF_XBC + CONV_DIM
IN_COLS = OFF_DT + N_DIR * N_HEADS

LANES = 128
SUBLANES = 8
BF16_ROWS = 16
VMEM_LIMIT_BYTES = 60 * 1024 * 1024

DT_COLS = N_BC_GROUPS * LANES

TM_PROJ = 512
TN_PROJ = 512
TM_OUT = 512
SSD_BWD_CHUNKS = 4
SSD_FWD_CHUNKS = 2
POOL_UNROLL = 2
CONV_HALO = BF16_ROWS
CONV_STRIDE = 4
CONV_BLK = CONV_STRIDE * SUBLANES


def _sigmoid(x):
    return 1.0 / (1.0 + jnp.exp(-x))


def _silu(x):
    return x * _sigmoid(x)


def _softplus(x):
    return jnp.maximum(x, 0.0) + jnp.log1p(jnp.exp(-jnp.abs(x)))


def _split3(a):
    hi = a.astype(BF16)
    r1 = a - hi.astype(F32)
    mid = r1.astype(BF16)
    lo = (r1 - mid.astype(F32)).astype(BF16)
    return hi, mid, lo


def _dot(a, b):
    return jnp.dot(a, b, preferred_element_type=F32)


def _dot3(a_f32, b_bf16):
    hi, mid, lo = _split3(a_f32)
    return _dot(hi, b_bf16) + _dot(mid, b_bf16) + _dot(lo, b_bf16)


def _mod_kernel(c_ref, w_ref, b_ref, o_ref):
    s = _silu(c_ref[...])
    o_ref[...] = _dot(s, w_ref[...]) + b_ref[...]


def _mod_call(cc, w_ada, b_ada):
    rows = cc.shape[0]
    tn = 512
    return pl.pallas_call(
        _mod_kernel,
        grid=(3 * D_MODEL // tn,),
        in_specs=[
            pl.BlockSpec((rows, D_MODEL), lambda j: (0, 0)),
            pl.BlockSpec((D_MODEL, tn), lambda j: (0, j)),
            pl.BlockSpec((1, tn), lambda j: (0, j)),
        ],
        out_specs=pl.BlockSpec((rows, tn), lambda j: (0, j)),
        out_shape=jax.ShapeDtypeStruct((rows, 3 * D_MODEL), F32),
        name="mod",
    )(cc, w_ada, b_ada)


def _norm_modulate(x, wpre, scale, shift):
    ms = jnp.mean(x * x, axis=-1, keepdims=True)
    xn = x * lax.rsqrt(ms + NORM_EPS)
    return (xn * wpre) * (1.0 + scale) + shift


def _dt_epilogue(acc, dtb, alog, dt_ref, acs_ref, tm):
    lane = lax.broadcasted_iota(jnp.int32, (1, DT_COLS), 1) & (LANES - 1)
    used = lane < 2 * HEADS_PER_GROUP
    is_bwd = used & (lane >= HEADS_PER_GROUP)
    dt = _softplus(acc + dtb)
    a = dt * jnp.where(used, -jnp.exp(alog), 0.0)
    dt_ref[0] = dt
    ri = lax.broadcasted_iota(jnp.int32, (CHUNK, CHUNK), 0)
    ci = lax.broadcasted_iota(jnp.int32, (CHUNK, CHUNK), 1)
    tri = (ri >= ci).astype(BF16)
    for q in range(tm // CHUNK):
        aq = a[q * CHUNK:(q + 1) * CHUNK]
        fwd = _dot3_lhs01(tri, aq)
        tot = fwd[CHUNK - 1:CHUNK]
        bwd = tot - fwd + aq
        acs_ref[0, q * CHUNK:(q + 1) * CHUNK, :] = jnp.where(is_bwd, bwd, fwd)


def _dot3_lhs01(m01_bf16, a_f32):
    hi, mid, lo = _split3(a_f32)
    return _dot(m01_bf16, hi) + _dot(m01_bf16, mid) + _dot(m01_bf16, lo)


def _conv_silu_segment(raw_s, cs_s, cw_ref, cb_ref, col0, tm):
    n_ld = CONV_STRIDE + CONV_K - 1
    for lc in range(TN_PROJ // LANES):
        pcols = slice(col0 + lc * LANES, col0 + (lc + 1) * LANES)
        wk = [jnp.broadcast_to(cw_ref[k:k + 1, pcols], (SUBLANES, LANES)) for k in range(CONV_K)]
        bb = jnp.broadcast_to(cb_ref[:, pcols], (SUBLANES, LANES))
        for blk in range(tm // CONV_BLK):
            base = CONV_HALO + blk * CONV_BLK - CONV_LEFT
            lds = [raw_s[lc, base + o:base + o + CONV_BLK:CONV_STRIDE, :] for o in range(n_ld)]
            for j in range(CONV_STRIDE):
                acc = lds[j] * wk[0]
                for k in range(1, CONV_K):
                    acc = acc + lds[j + k] * wk[k]
                acc = acc + bb
                cs_s[lc, blk * CONV_BLK + j:(blk + 1) * CONV_BLK:CONV_STRIDE, :] = _silu(acc)


def _stage_rows(x_ref, xp_ref, xn_ref, sh_ref, sc_ref, wpre_ref, hb_s, tm):
    i = pl.program_id(1)
    mp = (i > 0).astype(F32)
    mn = (i < pl.num_programs(1) - 1).astype(F32)
    wpre, sc, sh = wpre_ref[...], sc_ref[0], sh_ref[0]
    hb_s[0:CONV_HALO, :] = (_norm_modulate(xp_ref[0], wpre, sc, sh) * mp).astype(BF16)
    hb_s[CONV_HALO:CONV_HALO + tm, :] = _norm_modulate(x_ref[0], wpre, sc, sh).astype(BF16)
    hb_s[CONV_HALO + tm:2 * CONV_HALO + tm, :] = (
        _norm_modulate(xn_ref[0], wpre, sc, sh) * mn).astype(BF16)


def _xbc_segments(hb_s, w_ref, cw_ref, cb_ref, xbc_ref, raw_s, cs_s, tm):
    for s in range(CONV_DIM // TN_PROJ):
        col = s * TN_PROJ
        slot = s % 2
        acc = _dot(hb_s[...], w_ref[:, OFF_XBC + col:OFF_XBC + col + TN_PROJ])
        for lc in range(TN_PROJ // LANES):
            raw_s[slot, lc] = acc[:, lc * LANES:(lc + 1) * LANES]
        _conv_silu_segment(raw_s.at[slot], cs_s.at[slot], cw_ref, cb_ref, col, tm)
        for lc in range(TN_PROJ // LANES):
            xbc_ref[0, :, col + lc * LANES:col + (lc + 1) * LANES] = cs_s[slot, lc].astype(BF16)


def _inproj_main_kernel(x_ref, xp_ref, xn_ref, sh_ref, sc_ref, wpre_ref, w_ref, wdt_ref, bm_ref,
                        cw_ref, cb_ref, dtb_ref, alog_ref,
                        v_ref, zp_ref, zs_ref, gt_ref, xbc_ref, dt_ref, acs_ref,
                        hb_s, raw_s, cs_s, *, tm):
    _stage_rows(x_ref, xp_ref, xn_ref, sh_ref, sc_ref, wpre_ref, hb_s, tm)
    hb = hb_s[CONV_HALO:CONV_HALO + tm, :]
    for s in range(OFF_XBC // TN_PROJ):
        col = s * TN_PROJ
        acc = _dot(hb, w_ref[:, col:col + TN_PROJ])
        if col < OFF_POOL_Z:
            v_ref[0, :, col - OFF_POOL_V:col - OFF_POOL_V + TN_PROJ] = acc.astype(BF16)
        elif col < OFF_SSD_Z:
            zp_ref[0, :, col - OFF_POOL_Z:col - OFF_POOL_Z + TN_PROJ] = _silu(acc).astype(BF16)
        elif col < OFF_GATE:
            zs_ref[0, :, col - OFF_SSD_Z:col - OFF_SSD_Z + TN_PROJ] = _silu(acc).astype(BF16)
        else:
            o = col - OFF_GATE
            gt_ref[0, :, o:o + TN_PROJ] = _sigmoid(acc + bm_ref[:, o:o + TN_PROJ]).astype(BF16)
    _xbc_segments(hb_s, w_ref, cw_ref, cb_ref, xbc_ref, raw_s, cs_s, tm)
    acc = _dot(hb, wdt_ref[...])
    _dt_epilogue(acc, dtb_ref[...], alog_ref[...], dt_ref, acs_ref, tm)


def _inproj_ctx_kernel(x_ref, xp_ref, xn_ref, sh_ref, sc_ref, wpre_ref, w_ref, wdt_ref,
                       cw_ref, cb_ref, dtb_ref, alog_ref,
                       xbc_ref, dt_ref, acs_ref, hb_s, raw_s, cs_s, *, tm):
    _stage_rows(x_ref, xp_ref, xn_ref, sh_ref, sc_ref, wpre_ref, hb_s, tm)
    _xbc_segments(hb_s, w_ref, cw_ref, cb_ref, xbc_ref, raw_s, cs_s, tm)
    acc = _dot(hb_s[CONV_HALO:CONV_HALO + tm, :], wdt_ref[...])
    _dt_epilogue(acc, dtb_ref[...], alog_ref[...], dt_ref, acs_ref, tm)


def _resident(shape):
    nd = len(shape)
    return pl.BlockSpec(shape, lambda *_: (0,) * nd, pipeline_mode=pl.Buffered(1))


def _inproj_call(xin, mod3, mod_row, wpre, w_main, w_dt, b_merge, conv_w, conv_b, dtb_lay, alog_lay,
                 *, ctx):
    bn, ln, _ = xin.shape
    tm = min(TM_PROJ, ln)
    grid = (bn, ln // tm)
    halo_per_blk = tm // CONV_HALO
    n_halo_blk = ln // CONV_HALO
    if mod_row is None:
        row = lambda b, i: b
    else:
        row = lambda b, i: mod_row
    tok = lambda width: pl.BlockSpec((1, tm, width), lambda b, i: (b, i, 0))
    in_specs = [
        tok(D_MODEL),
        pl.BlockSpec((1, CONV_HALO, D_MODEL),
                     lambda b, i: (b, jnp.maximum(i * halo_per_blk - 1, 0), 0)),
        pl.BlockSpec((1, CONV_HALO, D_MODEL),
                     lambda b, i: (b, jnp.minimum((i + 1) * halo_per_blk, n_halo_blk - 1), 0)),
        pl.BlockSpec((1, 1, D_MODEL), lambda b, i: (row(b, i), 0, 0)),
        pl.BlockSpec((1, 1, D_MODEL), lambda b, i: (row(b, i), 0, 1)),
        _resident((1, D_MODEL)),
        _resident(w_main.shape),
        _resident(w_dt.shape),
    ]
    args = [xin, xin, xin, mod3, mod3, wpre, w_main, w_dt]
    if not ctx:
        in_specs.append(_resident(b_merge.shape))
        args.append(b_merge)
    in_specs += [_resident(conv_w.shape), _resident(conv_b.shape),
                 _resident(dtb_lay.shape), _resident(alog_lay.shape)]
    args += [conv_w, conv_b, dtb_lay, alog_lay]
    tail_specs = [tok(CONV_DIM), tok(DT_COLS), tok(DT_COLS)]
    tail_shapes = [
        jax.ShapeDtypeStruct((bn, ln, CONV_DIM), BF16),
        jax.ShapeDtypeStruct((bn, ln, DT_COLS), F32),
        jax.ShapeDtypeStruct((bn, ln, DT_COLS), F32),
    ]
    if ctx:
        body = functools.partial(_inproj_ctx_kernel, tm=tm)
        out_specs, out_shape = tail_specs, tail_shapes
    else:
        body = functools.partial(_inproj_main_kernel, tm=tm)
        out_specs = [tok(D_MODEL), tok(D_MODEL), tok(D_INNER), tok(2 * D_MODEL)] + tail_specs
        out_shape = [
            jax.ShapeDtypeStruct((bn, ln, D_MODEL), BF16),
            jax.ShapeDtypeStruct((bn, ln, D_MODEL), BF16),
            jax.ShapeDtypeStruct((bn, ln, D_INNER), BF16),
            jax.ShapeDtypeStruct((bn, ln, 2 * D_MODEL), BF16),
        ] + tail_shapes
    return pl.pallas_call(
        body,
        grid=grid,
        in_specs=in_specs,
        out_specs=out_specs,
        out_shape=out_shape,
        scratch_shapes=[
            pltpu.VMEM((tm + 2 * CONV_HALO, D_MODEL), BF16),
            pltpu.VMEM((2, TN_PROJ // LANES, tm + 2 * CONV_HALO, LANES), F32),
            pltpu.VMEM((2, TN_PROJ // LANES, tm, LANES), F32),
        ],
        compiler_params=pltpu.CompilerParams(
            dimension_semantics=("parallel", "parallel"), vmem_limit_bytes=VMEM_LIMIT_BYTES),
        name="inproj_ctx" if ctx else "inproj",
    )(*args)


def _ssd_kernel(*refs, nc, with_output, has_h0):
    it = iter(refs)
    x_ref, b_ref, c_ref, dt_ref, acs_ref = (next(it) for _ in range(5))
    h0_ref = next(it) if has_h0 else None
    if with_output:
        z_ref, dsk_ref, nw_ref = (next(it) for _ in range(3))
        y_ref = next(it)
    else:
        st_ref = next(it)
    ef_s, eb_s, hf_s, hb_s = (next(it) for _ in range(4))
    hbin_s = next(it) if with_output else None

    gi = GROUP_INNER
    hpg = HEADS_PER_GROUP

    er = lax.broadcasted_iota(jnp.int32, (LANES, gi), 0)
    el = lax.broadcasted_iota(jnp.int32, (LANES, gi), 1) // HEAD_DIM
    ef_s[...] = (er == el).astype(BF16)
    eb_s[...] = (er == el + hpg).astype(BF16)

    lane = lax.broadcasted_iota(jnp.int32, (1, LANES), 1)
    is_bwd = (lane >= hpg) & (lane < 2 * hpg)

    def chunk_state_terms(c):
        r0 = pl.multiple_of(c * CHUNK, CHUNK)
        dtc = dt_ref[0, pl.ds(r0, CHUNK), :]
        acs = acs_ref[0, pl.ds(r0, CHUNK), :]
        tot = jnp.where(is_bwd, acs[0:1, :], acs[CHUNK - 1:CHUNK, :])
        w = dtc * jnp.exp(tot - acs)
        return r0, dtc, acs, tot, w

    def expand(parts, e_s):
        full = _dot(jnp.concatenate([p.astype(BF16) for p in parts], axis=0), e_s[...])
        return [full[k * CHUNK:(k + 1) * CHUNK] for k in range(len(parts))]

    def local_state(r0, wx):
        xw = x_ref[0, pl.ds(r0, CHUNK), :].astype(F32) * wx
        bc = b_ref[0, pl.ds(r0, CHUNK), :]
        return lax.dot_general(bc, xw.astype(BF16), (((0,), (0,)), ((), ())),
                               preferred_element_type=F32)

    def chunk_decays(tots, e_s):
        t8 = jnp.concatenate([jnp.broadcast_to(t, (SUBLANES, LANES)) for t in tots], axis=0)
        d = jnp.exp(_dot3(t8, e_s[...]))
        return [d[k * SUBLANES:k * SUBLANES + 1, :] for k in range(len(tots))]

    if has_h0:
        hf_s[...] = h0_ref[0, 0, 0]
        hb_s[...] = h0_ref[0, 0, 1]
    else:
        hf_s[...] = jnp.zeros_like(hf_s)
        hb_s[...] = jnp.zeros_like(hb_s)


    ub = min(SSD_BWD_CHUNKS, nc)

    def bwd_body(i, carry):
        chunks = [nc - 1 - (i * ub + u) for u in range(ub)]
        terms = [chunk_state_terms(c) for c in chunks]
        wxs = expand([t[4] for t in terms], eb_s)
        decs = chunk_decays([t[3] for t in terms], eb_s)
        locs = [local_state(t[0], wx) for t, wx in zip(terms, wxs)]
        hb = hb_s[...]
        for c, s_loc, dec in zip(chunks, locs, decs):
            if with_output:
                hbin_s[c] = hb.astype(BF16)
            hb = hb * dec + s_loc
        hb_s[...] = hb
        return carry

    lax.fori_loop(0, nc // ub, bwd_body, 0)

    ri = lax.broadcasted_iota(jnp.int32, (CHUNK, CHUNK), 0)
    ci = lax.broadcasted_iota(jnp.int32, (CHUNK, CHUNK), 1)
    lower = ri > ci
    upper = ri < ci
    pair_lane = lax.broadcasted_iota(jnp.int32, (CHUNK, 2 * HEAD_DIM), 1)
    first_head = pair_lane < HEAD_DIM

    def intra_chunk(xc, cb, acs, dtc):
        acs_t = acs.T
        dt_t = dtc.T
        pieces = []
        for p in range(hpg // 2):
            ms = []
            for r in (2 * p, 2 * p + 1):
                colf = acs[:, r:r + 1]
                colb = acs[:, hpg + r:hpg + r + 1]
                rowf = acs_t[r:r + 1, :]
                rowb = acs_t[hpg + r:hpg + r + 1, :]
                dtf = dt_t[r:r + 1, :]
                dtb = dt_t[hpg + r:hpg + r + 1, :]
                arg = jnp.where(lower, colf, colb) - jnp.where(lower, rowf, rowb)
                wrow = jnp.where(lower, dtf, jnp.where(upper, dtb, dtf + dtb))
                ms.append((cb * jnp.exp(arg) * wrow).astype(BF16))
            xp = xc[:, p * 2 * HEAD_DIM:(p + 1) * 2 * HEAD_DIM]
            zero = jnp.zeros_like(xp)
            rhs = jnp.concatenate([jnp.where(first_head, xp, zero),
                                   jnp.where(first_head, zero, xp)], axis=0)
            pieces.append(_dot(jnp.concatenate(ms, axis=1), rhs))
        return jnp.concatenate(pieces, axis=1)

    uf = min(SSD_FWD_CHUNKS, nc)

    def fwd_body(i, carry):
        chunks = [i * uf + u for u in range(uf)]
        terms = [chunk_state_terms(c) for c in chunks]
        decs = chunk_decays([t[3] for t in terms], ef_s)
        if with_output:
            eacs = [jnp.exp(t[2]) for t in terms]
            ex_f = expand(eacs + [t[4] for t in terms], ef_s)
            exfs, wxs = ex_f[:uf], ex_f[uf:]
            exbs = expand(eacs, eb_s)
            pre = []
            for c, (r0, dtc, acs, tot, w), exf, exb in zip(chunks, terms, exfs, exbs):
                xc = x_ref[0, pl.ds(r0, CHUNK), :]
                cc = c_ref[0, pl.ds(r0, CHUNK), :]
                cb = lax.dot_general(cc, b_ref[0, pl.ds(r0, CHUNK), :], (((1,), (1,)), ((), ())),
                                     preferred_element_type=F32)
                y = _dot(cc, hbin_s[c]) * exb
                y = y + dsk_ref[...] * xc.astype(F32)
                pre.append((xc, cc, cb, exf, y))
        else:
            wxs = expand([t[4] for t in terms], ef_s)
        locs = [local_state(t[0], wx) for t, wx in zip(terms, wxs)]
        if with_output:
            ys = [y + intra_chunk(xc, cb, acs, dtc)
                  for (xc, cc, cb, exf, y), (r0, dtc, acs, tot, w) in zip(pre, terms)]
        hf = hf_s[...]
        for u in range(uf):
            if with_output:
                xc, cc, cb, exf, _ = pre[u]
                r0 = terms[u][0]
                y = ys[u] + _dot(cc, hf.astype(BF16)) * exf
                uu = y * z_ref[0, pl.ds(r0, CHUNK), :].astype(F32)
                ms_u = jnp.mean(uu * uu, axis=-1, keepdims=True)
                y_ref[0, pl.ds(r0, CHUNK), :] = (
                    (uu * lax.rsqrt(ms_u + NORM_EPS)) * nw_ref[...]).astype(BF16)
            hf = hf * decs[u] + locs[u]
        hf_s[...] = hf
        return carry

    lax.fori_loop(0, nc // uf, fwd_body, 0)

    if not with_output:
        st_ref[0, 0, 0] = hf_s[...]
        st_ref[0, 0, 1] = hb_s[...]


def _ssd_call(xbc, dt, acs, h0, z, dsk, nw):
    bn, ls, _ = xbc.shape
    nc = ls // CHUNK
    with_output = z is not None
    has_h0 = h0 is not None
    gi = GROUP_INNER
    n_x = D_INNER // D_STATE
    in_specs = [
        pl.BlockSpec((1, ls, gi), lambda b, g: (b, 0, g)),
        pl.BlockSpec((1, ls, D_STATE), lambda b, g: (b, 0, n_x + g)),
        pl.BlockSpec((1, ls, D_STATE), lambda b, g: (b, 0, n_x + N_BC_GROUPS + g)),
        pl.BlockSpec((1, ls, LANES), lambda b, g: (b, 0, g)),
        pl.BlockSpec((1, ls, LANES), lambda b, g: (b, 0, g)),
    ]
    args = [xbc, xbc, xbc, dt, acs]
    st_spec = pl.BlockSpec((1, 1, N_DIR, D_STATE, gi), lambda b, g: (b, g, 0, 0, 0))
    if has_h0:
        in_specs.append(st_spec)
        args.append(h0)
    if with_output:
        in_specs += [
            pl.BlockSpec((1, ls, gi), lambda b, g: (b, 0, g)),
            pl.BlockSpec((1, gi), lambda b, g: (0, g)),
            pl.BlockSpec((1, gi), lambda b, g: (0, g)),
        ]
        args += [z, dsk, nw]
        out_specs = pl.BlockSpec((1, ls, gi), lambda b, g: (b, 0, g))
        out_shape = jax.ShapeDtypeStruct((bn, ls, D_INNER), BF16)
    else:
        out_specs = st_spec
        out_shape = jax.ShapeDtypeStruct((bn, N_BC_GROUPS, N_DIR, D_STATE, gi), F32)
    scratch = [
        pltpu.VMEM((LANES, gi), BF16),
        pltpu.VMEM((LANES, gi), BF16),
        pltpu.VMEM((D_STATE, gi), F32),
        pltpu.VMEM((D_STATE, gi), F32),
    ]
    if with_output:
        scratch.append(pltpu.VMEM((nc, D_STATE, gi), BF16))
    return pl.pallas_call(
        functools.partial(_ssd_kernel, nc=nc, with_output=with_output, has_h0=has_h0),
        grid=(bn, N_BC_GROUPS),
        in_specs=in_specs,
        out_specs=out_specs,
        out_shape=out_shape,
        scratch_shapes=scratch,
        compiler_params=pltpu.CompilerParams(
            dimension_semantics=("parallel", "parallel"), vmem_limit_bytes=VMEM_LIMIT_BYTES),
        name="ssd" if with_output else "ssd_ctx",
    )(*args)


POOL_TB = 4 * GRID_W
POOL_PAD_ROWS = 8
POOL_MM_ROWS = 512


def _pool_kernel(v_ref, zp_ref, pw_ref, ps_ref, o_ref, cm_s, d_s, csc_s, *, ls):
    g = pl.program_id(1)
    rows = ls // GRID_W
    pad = POOL_PAD_ROWS * GRID_W
    pg = POOL_GROUP

    def body(k):
        lo, hi = k // 2, k - 1 - k // 2
        ti = lax.broadcasted_iota(jnp.int32, (POOL_TB, POOL_TB), 0)
        si = lax.broadcasted_iota(jnp.int32, (POOL_TB, POOL_TB), 1)
        dlt = si - ti
        band = ((ti // GRID_W) == (si // GRID_W)) & (dlt >= -lo) & (dlt <= hi)
        pmat = band.astype(BF16)
        tc = lax.broadcasted_iota(jnp.int32, (POOL_TB, pg), 0) % GRID_W
        cnt_c = jnp.minimum(tc + hi, GRID_W - 1) - jnp.maximum(tc - lo, 0) + 1
        csc_s[...] = 1.0 / cnt_c.astype(F32)
        cm_s[0:pad, :] = jnp.zeros((pad, pg), F32)
        cm_s[pad + ls:2 * pad + ls, :] = jnp.zeros((pad, pg), F32)

        def col_pass(tb, carry):
            r0 = pl.multiple_of(tb * POOL_TB, POOL_TB)
            cm = _dot(pmat, v_ref[0, pl.ds(r0, POOL_TB), :]) * csc_s[...]
            cm_s[pl.ds(pad + r0, POOL_TB), :] = cm
            return carry

        lax.fori_loop(0, ls // POOL_TB, col_pass, 0, unroll=POOL_UNROLL)

        def row_pass(r, carry):
            r0 = pl.multiple_of(r * GRID_W, GRID_W)
            acc = cm_s[pl.ds(pad + r0 - lo * GRID_W, GRID_W), :]
            for o in range(-lo + 1, hi + 1):
                acc = acc + cm_s[pl.ds(pad + r0 + o * GRID_W, GRID_W), :]
            cnt_r = jnp.minimum(r + hi, rows - 1) - jnp.maximum(r - lo, 0) + 1
            inv_r = 1.0 / jnp.full((1, pg), cnt_r, jnp.int32).astype(F32)
            m = acc * inv_r
            d = m - v_ref[0, pl.ds(r0, GRID_W), :].astype(F32)
            d_s[pl.ds(r0, GRID_W), :] = d.astype(BF16)
            return carry

        lax.fori_loop(0, rows, row_pass, 0, unroll=POOL_UNROLL)

        def mm(tb, carry):
            r0 = pl.multiple_of(tb * POOL_MM_ROWS, POOL_MM_ROWS)
            y = _dot(d_s[pl.ds(r0, POOL_MM_ROWS), :], pw_ref[0])
            y = (y * ps_ref[...]) * zp_ref[0, pl.ds(r0, POOL_MM_ROWS), :].astype(F32)
            o_ref[0, pl.ds(r0, POOL_MM_ROWS), :] = y.astype(BF16)
            return carry

        lax.fori_loop(0, ls // POOL_MM_ROWS, mm, 0, unroll=POOL_UNROLL)

    for gidx, k in enumerate(POOL_WINDOWS):
        pl.when(g == gidx)(functools.partial(body, k))


def _pool_call(v, zp, pool_w, pool_scale):
    bn, ls, _ = v.shape
    pg = POOL_GROUP
    tok = pl.BlockSpec((1, ls, pg), lambda b, g: (b, 0, g))
    return pl.pallas_call(
        functools.partial(_pool_kernel, ls=ls),
        grid=(bn, N_POOL_GROUPS),
        in_specs=[
            tok,
            tok,
            pl.BlockSpec((1, pg, pg), lambda b, g: (g, 0, 0)),
            pl.BlockSpec((1, pg), lambda b, g: (0, g)),
        ],
        out_specs=tok,
        out_shape=jax.ShapeDtypeStruct((bn, ls, D_MODEL), BF16),
        scratch_shapes=[
            pltpu.VMEM((ls + 2 * POOL_PAD_ROWS * GRID_W, pg), F32),
            pltpu.VMEM((ls, pg), BF16),
            pltpu.VMEM((POOL_TB, pg), F32),
        ],
        compiler_params=pltpu.CompilerParams(
            dimension_semantics=("parallel", "parallel"), vmem_limit_bytes=VMEM_LIMIT_BYTES),
        name="pool",
    )(v, zp, pool_w, pool_scale)


def _out_kernel(yp_ref, ys_ref, gt_ref, x_ref, gate_ref, wpost_ref, wpp_ref, wps_ref, wo_ref, o_ref):
    a = _dot(yp_ref[0], wpp_ref[...])
    b = _dot(ys_ref[0], wps_ref[...])
    merged = (gt_ref[0, :, 0:D_MODEL].astype(F32) * a
              + gt_ref[0, :, D_MODEL:2 * D_MODEL].astype(F32) * b)
    o = _dot(merged.astype(BF16), wo_ref[...])
    ms = jnp.mean(o * o, axis=-1, keepdims=True)
    on = (o * lax.rsqrt(ms + NORM_EPS)) * wpost_ref[...]
    o_ref[0] = x_ref[0] + gate_ref[0] * on


def _out_call(yp, ys, gates, x, mod3, wpost, wpp, wps, wo):
    bn, ls, _ = x.shape
    tm = TM_OUT
    tok = lambda width: pl.BlockSpec((1, tm, width), lambda b, i: (b, i, 0))
    return pl.pallas_call(
        _out_kernel,
        grid=(bn, ls // tm),
        in_specs=[
            tok(D_MODEL), tok(D_INNER), tok(2 * D_MODEL), tok(D_MODEL),
            pl.BlockSpec((1, 1, D_MODEL), lambda b, i: (b, 0, 2)),
            _resident(wpost.shape), _resident(wpp.shape), _resident(wps.shape), _resident(wo.shape),
        ],
        out_specs=tok(D_MODEL),
        out_shape=jax.ShapeDtypeStruct(x.shape, F32),
        compiler_params=pltpu.CompilerParams(
            dimension_semantics=("parallel", "parallel"), vmem_limit_bytes=VMEM_LIMIT_BYTES),
        name="out",
    )(yp, ys, gates, x, mod3, wpost, wpp, wps, wo)


def _dt_layout_index():
    idx = []
    zero_col = N_DIR * N_HEADS
    for g in range(N_BC_GROUPS):
        blk = [g * HEADS_PER_GROUP + r for r in range(HEADS_PER_GROUP)]
        blk += [N_HEADS + g * HEADS_PER_GROUP + r for r in range(HEADS_PER_GROUP)]
        blk += [zero_col] * (LANES - len(blk))
        idx += blk
    return jnp.asarray(idx, jnp.int32)


def _dt_layout(cols):
    padded = jnp.concatenate([cols, jnp.zeros((cols.shape[0], 1), cols.dtype)], axis=1)
    return jnp.take(padded, _dt_layout_index(), axis=1)


def kernel(x, c, ctx, c_ctx, w_ada, b_ada, norm_pre, norm_post, w_in, b_merge, pool_w, pool_scale,
           conv_w, conv_b, dt_bias, a_log, d_skip, ssd_norm, w_proj_pool, w_proj_ssd, w_out):
    assert w_ada.shape[0] == 1, "single-layer block"
    bn = x.shape[0]
    assert x.shape[1] % (GRID_W * GRID_W) == 0 and x.shape[1] // GRID_W == GRID_W

    n_cond = bn + 1
    cond_rows = -(-n_cond // SUBLANES) * SUBLANES
    cc = jnp.concatenate([c, c_ctx[None, :], jnp.zeros((cond_rows - n_cond, D_MODEL), F32)], axis=0)
    mod = _mod_call(cc, w_ada[0], b_ada[0][None, :])
    mod3 = mod.reshape(cond_rows, 1, 3 * D_MODEL)

    w_in0 = w_in[0]
    w_main = w_in0[:, :OFF_DT].astype(BF16)
    w_dt = _dt_layout(w_in0[:, OFF_DT:]).astype(BF16)
    dtb_lay = _dt_layout(dt_bias[0].reshape(1, N_DIR * N_HEADS))
    alog_lay = _dt_layout(a_log[0].reshape(1, N_DIR * N_HEADS))
    wpre = norm_pre[0][None, :]
    wpost = norm_post[0][None, :]
    cw = conv_w[0]
    cb = conv_b[0][None, :]
    dsk = jnp.repeat(d_skip[0], HEAD_DIM)[None, :]
    nw = ssd_norm[0][None, :]

    xbc_c, dt_c, acs_c = _inproj_call(ctx, mod3, bn, wpre, w_main, w_dt, None, cw, cb,
                                      dtb_lay, alog_lay, ctx=True)
    h0 = _ssd_call(xbc_c, dt_c, acs_c, None, None, None, None)

    v, zp, zs, gates, xbc, dt, acs = _inproj_call(
        x, mod3, None, wpre, w_main, w_dt, b_merge[0][None, :], cw, cb, dtb_lay, alog_lay,
        ctx=False)
    y_pool = _pool_call(v, zp, pool_w[0].astype(BF16), pool_scale[0][None, :])
    y_ssd = _ssd_call(xbc, dt, acs, h0, zs, dsk, nw)
    return _out_call(y_pool, y_ssd, gates, x, mod3, wpost,
                     w_proj_pool[0].astype(BF16), w_proj_ssd[0].astype(BF16), w_out[0].astype(BF16))
```

```python
import functools

import jax
import jax.numpy as jnp
from jax import lax
from jax.experimental import pallas as pl
from jax.experimental.pallas import tpu as pltpu

F32 = jnp.float32
BF16 = jnp.bfloat16

D_MODEL = 1024
GRID_W = 64
NORM_EPS = 1e-6
LOG2_E = 1.4426950408889634
POOL_WINDOWS = (2, 4, 8, 16)
N_POOL_GROUPS = 4
POOL_GROUP = D_MODEL // N_POOL_GROUPS
D_INNER = 2 * D_MODEL
HEAD_DIM = 64
N_HEADS = D_INNER // HEAD_DIM
D_STATE = 128
N_BC_GROUPS = 4
HEADS_PER_GROUP = N_HEADS // N_BC_GROUPS
GROUP_INNER = D_INNER // N_BC_GROUPS
CONV_K = 4
CONV_LEFT = CONV_K // 2
CHUNK = 128
N_DIR = 2
CONV_DIM = D_INNER + 2 * N_BC_GROUPS * D_STATE
OFF_POOL_V = 0
OFF_POOL_Z = OFF_POOL_V + D_MODEL
OFF_SSD_Z = OFF_POOL_Z + D_MODEL
OFF_GATE = OFF_SSD_Z + D_INNER
OFF_XBC = OFF_GATE + 2 * D_MODEL
OFF_DT = OFF_XBC + CONV_DIM
IN_COLS = OFF_DT + N_DIR * N_HEADS

LANES = 128
SUBLANES = 8
BF16_ROWS = 16
VMEM_LIMIT_BYTES = 60 * 1024 * 1024

DT_COLS = N_BC_GROUPS * LANES

TM_PROJ = 512
TN_PROJ = 256
TM_OUT = 512
SSD_BWD_CHUNKS = 4
SSD_FWD_CHUNKS = 2
SSD_STAGE_SLOTS = 2
SSD_STAGE_BUFS = 5
POOL_UNROLL = 2
CONV_HALO = BF16_ROWS
CONV_STRIDE = 4
CONV_BLK = CONV_STRIDE * SUBLANES
CONV_SLOTS = 2


def _sigmoid(x):
    return 1.0 / (1.0 + jnp.exp(-x))


def _silu(x):
    return x * _sigmoid(x)


def _softplus(x):
    return jnp.maximum(x, 0.0) + jnp.log1p(jnp.exp(-jnp.abs(x)))


def _split3(a):
    hi = a.astype(BF16)
    r1 = a - hi.astype(F32)
    mid = r1.astype(BF16)
    lo = (r1 - mid.astype(F32)).astype(BF16)
    return hi, mid, lo


def _dot(a, b):
    return jnp.dot(a, b, preferred_element_type=F32)


def _dot3(a_f32, b_bf16):
    hi, mid, lo = _split3(a_f32)
    return _dot(hi, b_bf16) + _dot(mid, b_bf16) + _dot(lo, b_bf16)


def _mod_kernel(c_ref, w_ref, b_ref, o_ref):
    s = _silu(c_ref[...])
    o_ref[...] = _dot(s, w_ref[...]) + b_ref[...]


def _mod_call(cc, w_ada, b_ada):
    rows = cc.shape[0]
    tn = 512
    return pl.pallas_call(
        _mod_kernel,
        grid=(3 * D_MODEL // tn,),
        in_specs=[
            pl.BlockSpec((rows, D_MODEL), lambda j: (0, 0)),
            pl.BlockSpec((D_MODEL, tn), lambda j: (0, j)),
            pl.BlockSpec((1, tn), lambda j: (0, j)),
        ],
        out_specs=pl.BlockSpec((rows, tn), lambda j: (0, j)),
        out_shape=jax.ShapeDtypeStruct((rows, 3 * D_MODEL), F32),
        name="mod",
    )(cc, w_ada, b_ada)


def _norm_modulate(x, wpre, scale, shift):
    ms = jnp.mean(x * x, axis=-1, keepdims=True)
    xn = x * lax.rsqrt(ms + NORM_EPS)
    return (xn * wpre) * (1.0 + scale) + shift


def _dt_epilogue(acc, dtb, alog, dt_ref, acs_ref, tm):
    lane = lax.broadcasted_iota(jnp.int32, (1, DT_COLS), 1) & (LANES - 1)
    used = lane < 2 * HEADS_PER_GROUP
    is_bwd = used & (lane >= HEADS_PER_GROUP)
    dt = _softplus(acc + dtb)
    a = dt * jnp.where(used, -LOG2_E * jnp.exp(alog), 0.0)
    dt_ref[0] = dt
    ri = lax.broadcasted_iota(jnp.int32, (CHUNK, CHUNK), 0)
    ci = lax.broadcasted_iota(jnp.int32, (CHUNK, CHUNK), 1)
    tri = (ri >= ci).astype(BF16)
    for q in range(tm // CHUNK):
        aq = a[q * CHUNK:(q + 1) * CHUNK]
        fwd = _dot3_lhs01(tri, aq)
        tot = fwd[CHUNK - 1:CHUNK]
        bwd = tot - fwd + aq
        acs_ref[0, q * CHUNK:(q + 1) * CHUNK, :] = jnp.where(is_bwd, bwd, fwd)


def _dot3_lhs01(m01_bf16, a_f32):
    hi, mid, lo = _split3(a_f32)
    return _dot(m01_bf16, hi) + _dot(m01_bf16, mid) + _dot(m01_bf16, lo)


def _conv_silu_segment(raw_s, cs_s, cw_ref, cb_ref, col0, tm):
    n_ld = CONV_STRIDE + CONV_K - 1
    for lc in range(TN_PROJ // LANES):
        pcols = slice(col0 + lc * LANES, col0 + (lc + 1) * LANES)
        wk = [jnp.broadcast_to(cw_ref[k:k + 1, pcols], (SUBLANES, LANES)) for k in range(CONV_K)]
        bb = jnp.broadcast_to(cb_ref[:, pcols], (SUBLANES, LANES))
        for blk in range(tm // CONV_BLK):
            base = CONV_HALO + blk * CONV_BLK - CONV_LEFT
            lds = [raw_s[lc][base + o:base + o + CONV_BLK:CONV_STRIDE, :] for o in range(n_ld)]
            for j in range(CONV_STRIDE):
                acc = lds[j] * wk[0]
                for k in range(1, CONV_K):
                    acc = acc + lds[j + k] * wk[k]
                acc = acc + bb
                cs_s[lc][blk * CONV_BLK + j:(blk + 1) * CONV_BLK:CONV_STRIDE, :] = _silu(acc)


def _stage_rows(x_ref, xp_ref, xn_ref, sh_ref, sc_ref, wpre_ref, hb_s, tm):
    i = pl.program_id(1)
    mp = (i > 0).astype(F32)
    mn = (i < pl.num_programs(1) - 1).astype(F32)
    wpre, sc, sh = wpre_ref[...], sc_ref[0], sh_ref[0]
    hb_s[0:CONV_HALO, :] = (_norm_modulate(xp_ref[0], wpre, sc, sh) * mp).astype(BF16)
    hb_s[CONV_HALO:CONV_HALO + tm, :] = _norm_modulate(x_ref[0], wpre, sc, sh).astype(BF16)
    hb_s[CONV_HALO + tm:2 * CONV_HALO + tm, :] = (
        _norm_modulate(xn_ref[0], wpre, sc, sh) * mn).astype(BF16)


def _slot_buffers(conv_scratch, k):
    n_lc = TN_PROJ // LANES
    slot = conv_scratch[(k % CONV_SLOTS) * 2 * n_lc:(k % CONV_SLOTS + 1) * 2 * n_lc]
    return slot[:n_lc], slot[n_lc:]


def _run_segments(segments, matmul_fn, epilogue_fn):
    pending = []
    for k, seg in enumerate(segments):
        matmul_fn(k, seg)
        pending.append((k, seg))
        if len(pending) == CONV_SLOTS:
            epilogue_fn(*pending.pop(0))
    for k, seg in pending:
        epilogue_fn(k, seg)


def _segment_fns(hb_s, w_ref, bm_ref, cw_ref, cb_ref, outs, conv_scratch, tm):
    v_ref, zp_ref, zs_ref, gt_ref, xbc_ref = outs
    n_lc = TN_PROJ // LANES
    rows = slice(CONV_HALO, CONV_HALO + tm)

    def matmul_fn(k, seg):
        kind, col = seg
        raw_s, _ = _slot_buffers(conv_scratch, k)
        if kind == "conv":
            acc = _dot(hb_s[...], w_ref[:, OFF_XBC + col:OFF_XBC + col + TN_PROJ])
            for lc in range(n_lc):
                raw_s[lc][...] = acc[:, lc * LANES:(lc + 1) * LANES]
        else:
            acc = _dot(hb_s[rows, :], w_ref[:, col:col + TN_PROJ])
            for lc in range(n_lc):
                raw_s[lc][rows, :] = acc[:, lc * LANES:(lc + 1) * LANES]

    def epilogue_fn(k, seg):
        kind, col = seg
        raw_s, cs_s = _slot_buffers(conv_scratch, k)
        if kind == "conv":
            _conv_silu_segment(raw_s, cs_s, cw_ref, cb_ref, col, tm)
            for lc in range(n_lc):
                xbc_ref[0, :, col + lc * LANES:col + (lc + 1) * LANES] = cs_s[lc][...].astype(BF16)
            return
        for lc in range(n_lc):
            acc = raw_s[lc][rows, :]
            c0 = col + lc * LANES
            if c0 < OFF_POOL_Z:
                v_ref[0, :, c0 - OFF_POOL_V:c0 - OFF_POOL_V + LANES] = acc.astype(BF16)
            elif c0 < OFF_SSD_Z:
                zp_ref[0, :, c0 - OFF_POOL_Z:c0 - OFF_POOL_Z + LANES] = _silu(acc).astype(BF16)
            elif c0 < OFF_GATE:
                zs_ref[0, :, c0 - OFF_SSD_Z:c0 - OFF_SSD_Z + LANES] = _silu(acc).astype(BF16)
            else:
                o = c0 - OFF_GATE
                gt_ref[0, :, o:o + LANES] = _sigmoid(acc + bm_ref[:, o:o + LANES]).astype(BF16)

    return matmul_fn, epilogue_fn


def _inproj_main_kernel(x_ref, xp_ref, xn_ref, sh_ref, sc_ref, wpre_ref, w_ref, wdt_ref, bm_ref,
                        cw_ref, cb_ref, dtb_ref, alog_ref,
                        v_ref, zp_ref, zs_ref, gt_ref, xbc_ref, dt_ref, acs_ref,
                        hb_s, *conv_scratch, tm):
    _stage_rows(x_ref, xp_ref, xn_ref, sh_ref, sc_ref, wpre_ref, hb_s, tm)
    n_plain = OFF_XBC // TN_PROJ
    n_conv = CONV_DIM // TN_PROJ
    per_conv = n_plain // n_conv
    segments = []
    for q in range(n_conv):
        segments += [("plain", s * TN_PROJ) for s in range(q * per_conv, (q + 1) * per_conv)]
        segments.append(("conv", q * TN_PROJ))
    segments += [("plain", s * TN_PROJ) for s in range(n_conv * per_conv, n_plain)]
    _run_segments(segments, *_segment_fns(hb_s, w_ref, bm_ref, cw_ref, cb_ref,
                                          (v_ref, zp_ref, zs_ref, gt_ref, xbc_ref),
                                          conv_scratch, tm))
    acc = _dot(hb_s[CONV_HALO:CONV_HALO + tm, :], wdt_ref[...])
    _dt_epilogue(acc, dtb_ref[...], alog_ref[...], dt_ref, acs_ref, tm)


def _inproj_ctx_kernel(x_ref, xp_ref, xn_ref, sh_ref, sc_ref, wpre_ref, w_ref, wdt_ref,
                       cw_ref, cb_ref, dtb_ref, alog_ref,
                       xbc_ref, dt_ref, acs_ref, hb_s, *conv_scratch, tm):
    _stage_rows(x_ref, xp_ref, xn_ref, sh_ref, sc_ref, wpre_ref, hb_s, tm)
    segments = [("conv", s * TN_PROJ) for s in range(CONV_DIM // TN_PROJ)]
    _run_segments(segments, *_segment_fns(hb_s, w_ref, None, cw_ref, cb_ref,
                                          (None, None, None, None, xbc_ref), conv_scratch, tm))
    acc = _dot(hb_s[CONV_HALO:CONV_HALO + tm, :], wdt_ref[...])
    _dt_epilogue(acc, dtb_ref[...], alog_ref[...], dt_ref, acs_ref, tm)


def _resident(shape):
    nd = len(shape)
    return pl.BlockSpec(shape, lambda *_: (0,) * nd, pipeline_mode=pl.Buffered(1))


def _inproj_call(xin, mod3, mod_row, wpre, w_main, w_dt, b_merge, conv_w, conv_b, dtb_lay, alog_lay,
                 *, ctx):
    bn, ln, _ = xin.shape
    tm = min(TM_PROJ, ln)
    grid = (bn, ln // tm)
    halo_per_blk = tm // CONV_HALO
    n_halo_blk = ln // CONV_HALO
    if mod_row is None:
        row = lambda b, i: b
    else:
        row = lambda b, i: mod_row
    tok = lambda width: pl.BlockSpec((1, tm, width), lambda b, i: (b, i, 0))
    in_specs = [
        tok(D_MODEL),
        pl.BlockSpec((1, CONV_HALO, D_MODEL),
                     lambda b, i: (b, jnp.maximum(i * halo_per_blk - 1, 0), 0)),
        pl.BlockSpec((1, CONV_HALO, D_MODEL),
                     lambda b, i: (b, jnp.minimum((i + 1) * halo_per_blk, n_halo_blk - 1), 0)),
        pl.BlockSpec((1, 1, D_MODEL), lambda b, i: (row(b, i), 0, 0)),
        pl.BlockSpec((1, 1, D_MODEL), lambda b, i: (row(b, i), 0, 1)),
        _resident((1, D_MODEL)),
        _resident(w_main.shape),
        _resident(w_dt.shape),
    ]
    args = [xin, xin, xin, mod3, mod3, wpre, w_main, w_dt]
    if not ctx:
        in_specs.append(_resident(b_merge.shape))
        args.append(b_merge)
    in_specs += [_resident(conv_w.shape), _resident(conv_b.shape),
                 _resident(dtb_lay.shape), _resident(alog_lay.shape)]
    args += [conv_w, conv_b, dtb_lay, alog_lay]
    tail_specs = [tok(CONV_DIM), tok(DT_COLS), tok(DT_COLS)]
    tail_shapes = [
        jax.ShapeDtypeStruct((bn, ln, CONV_DIM), BF16),
        jax.ShapeDtypeStruct((bn, ln, DT_COLS), F32),
        jax.ShapeDtypeStruct((bn, ln, DT_COLS), F32),
    ]
    if ctx:
        body = functools.partial(_inproj_ctx_kernel, tm=tm)
        out_specs, out_shape = tail_specs, tail_shapes
    else:
        body = functools.partial(_inproj_main_kernel, tm=tm)
        out_specs = [tok(D_MODEL), tok(D_MODEL), tok(D_INNER), tok(2 * D_MODEL)] + tail_specs
        out_shape = [
            jax.ShapeDtypeStruct((bn, ln, D_MODEL), BF16),
            jax.ShapeDtypeStruct((bn, ln, D_MODEL), BF16),
            jax.ShapeDtypeStruct((bn, ln, D_INNER), BF16),
            jax.ShapeDtypeStruct((bn, ln, 2 * D_MODEL), BF16),
        ] + tail_shapes
    return pl.pallas_call(
        body,
        grid=grid,
        in_specs=in_specs,
        out_specs=out_specs,
        out_shape=out_shape,
        scratch_shapes=[pltpu.VMEM((tm + 2 * CONV_HALO, D_MODEL), BF16)] + CONV_SLOTS * (
            (TN_PROJ // LANES) * [pltpu.VMEM((tm + 2 * CONV_HALO, LANES), F32)]
            + (TN_PROJ // LANES) * [pltpu.VMEM((tm, LANES), F32)]),
        compiler_params=pltpu.CompilerParams(
            dimension_semantics=("parallel", "parallel"), vmem_limit_bytes=VMEM_LIMIT_BYTES),
        name="inproj_ctx" if ctx else "inproj",
    )(*args)


def _ssd_kernel(*refs, nc, with_output, has_h0):
    it = iter(refs)
    x_ref, b_ref, c_ref, dt_ref, acs_ref = (next(it) for _ in range(5))
    h0_ref = next(it) if has_h0 else None
    if with_output:
        z_ref, dsk_ref, nw_ref = (next(it) for _ in range(3))
        y_ref = next(it)
    else:
        st_ref = next(it)
    ef_s, eb_s, hf_s, hb_s = (next(it) for _ in range(4))
    hbin_s = next(it) if with_output else None
    if with_output:
        stage_s = [[tuple(next(it) for _ in range(SSD_STAGE_BUFS)) for _ in range(SSD_FWD_CHUNKS)]
                   for _ in range(SSD_STAGE_SLOTS)]

    gi = GROUP_INNER
    hpg = HEADS_PER_GROUP

    er = lax.broadcasted_iota(jnp.int32, (LANES, gi), 0)
    el = lax.broadcasted_iota(jnp.int32, (LANES, gi), 1) // HEAD_DIM
    ef_s[...] = (er == el).astype(BF16)
    eb_s[...] = (er == el + hpg).astype(BF16)

    lane = lax.broadcasted_iota(jnp.int32, (1, LANES), 1)
    is_bwd = (lane >= hpg) & (lane < 2 * hpg)

    def chunk_state_terms(c):
        r0 = pl.multiple_of(c * CHUNK, CHUNK)
        dtc = dt_ref[0, pl.ds(r0, CHUNK), :]
        acs = acs_ref[0, pl.ds(r0, CHUNK), :]
        tot = jnp.where(is_bwd, acs[0:1, :], acs[CHUNK - 1:CHUNK, :])
        w = dtc * jnp.exp2(tot - acs)
        return r0, dtc, acs, tot, w

    def expand(parts, e_s):
        full = _dot(jnp.concatenate([p.astype(BF16) for p in parts], axis=0), e_s[...])
        return [full[k * CHUNK:(k + 1) * CHUNK] for k in range(len(parts))]

    def local_state(r0, wx):
        xw = x_ref[0, pl.ds(r0, CHUNK), :] * wx.astype(BF16)
        bc = b_ref[0, pl.ds(r0, CHUNK), :]
        return lax.dot_general(bc, xw, (((0,), (0,)), ((), ())), preferred_element_type=F32)

    def chunk_decays(tots, e_s):
        t8 = jnp.concatenate([jnp.broadcast_to(t, (SUBLANES, LANES)) for t in tots], axis=0)
        d = jnp.exp2(_dot3(t8, e_s[...]))
        return [d[k * SUBLANES:k * SUBLANES + 1, :] for k in range(len(tots))]

    if has_h0:
        hf_s[...] = h0_ref[0, 0, 0]
        hb_s[...] = h0_ref[0, 0, 1]
    else:
        hf_s[...] = jnp.zeros_like(hf_s)
        hb_s[...] = jnp.zeros_like(hb_s)


    ub = min(SSD_BWD_CHUNKS, nc)

    def bwd_body(i, carry):
        chunks = [nc - 1 - (i * ub + u) for u in range(ub)]
        terms = [chunk_state_terms(c) for c in chunks]
        wxs = expand([t[4] for t in terms], eb_s)
        decs = chunk_decays([t[3] for t in terms], eb_s)
        locs = [local_state(t[0], wx) for t, wx in zip(terms, wxs)]
        hb = hb_s[...]
        for c, s_loc, dec in zip(chunks, locs, decs):
            if with_output:
                hbin_s[c] = hb.astype(BF16)
            hb = hb * dec + s_loc
        hb_s[...] = hb
        return carry

    lax.fori_loop(0, nc // ub, bwd_body, 0)

    ri = lax.broadcasted_iota(jnp.int32, (CHUNK, CHUNK), 0)
    ci = lax.broadcasted_iota(jnp.int32, (CHUNK, CHUNK), 1)
    lower = ri > ci
    eye_b = (ri == ci).astype(BF16)
    pair_lane = lax.broadcasted_iota(jnp.int32, (CHUNK, 2 * HEAD_DIM), 1)
    first_head = pair_lane < HEAD_DIM

    def intra_chunk(xc, cb, acs, dtc):
        lrow_t = (acs - jnp.log2(dtc)).T
        dt_t = dtc.T
        cbb = cb.astype(BF16)
        pieces = []
        for p in range(hpg // 2):
            ms = []
            for r in (2 * p, 2 * p + 1):
                colf = acs[:, r:r + 1]
                colb = acs[:, hpg + r:hpg + r + 1]
                rowf = lrow_t[r:r + 1, :]
                rowb = lrow_t[hpg + r:hpg + r + 1, :]
                arg = jnp.where(lower, colf, colb) - jnp.where(lower, rowf, rowb)
                dtf = dt_t[r:r + 1, :].astype(BF16)
                ms.append(cbb * (jnp.exp2(arg).astype(BF16) + eye_b * dtf))
            xp = xc[:, p * 2 * HEAD_DIM:(p + 1) * 2 * HEAD_DIM]
            zero = jnp.zeros_like(xp)
            rhs = jnp.concatenate([jnp.where(first_head, xp, zero),
                                   jnp.where(first_head, zero, xp)], axis=0)
            pieces.append(_dot(jnp.concatenate(ms, axis=1), rhs))
        return jnp.concatenate(pieces, axis=1)

    uf = min(SSD_FWD_CHUNKS, nc)

    n_grp = nc // uf

    def states_only_body(i, carry):
        chunks = [i * uf + u for u in range(uf)]
        terms = [chunk_state_terms(c) for c in chunks]
        decs = chunk_decays([t[3] for t in terms], ef_s)
        wxs = expand([t[4] for t in terms], ef_s)
        locs = [local_state(t[0], wx) for t, wx in zip(terms, wxs)]
        hf = hf_s[...]
        for u in range(uf):
            hf = hf * decs[u] + locs[u]
        hf_s[...] = hf
        return carry

    def stage_a(grp, slot):
        grp = jnp.minimum(grp, n_grp - 1)
        chunks = [grp * uf + u for u in range(uf)]
        terms = [chunk_state_terms(c) for c in chunks]
        decs = chunk_decays([t[3] for t in terms], ef_s)
        eacs = [jnp.exp2(t[2]) for t in terms]
        ex_f = expand(eacs + [t[4] for t in terms], ef_s)
        exfs, wxs = ex_f[:uf], ex_f[uf:]
        exbs = expand(eacs, eb_s)
        for u, (c, (r0, dtc, acs, tot, w)) in enumerate(zip(chunks, terms)):
            cb_s, exf_s, yp_s, loc_s, dec_s = stage_s[slot][u]
            xc = x_ref[0, pl.ds(r0, CHUNK), :]
            cc = c_ref[0, pl.ds(r0, CHUNK), :]
            cb_s[...] = lax.dot_general(cc, b_ref[0, pl.ds(r0, CHUNK), :],
                                        (((1,), (1,)), ((), ())), preferred_element_type=F32)
            exf_s[...] = exfs[u].astype(BF16)
            yp_s[...] = _dot(cc, hbin_s[c]) * exbs[u] + dsk_ref[...] * xc.astype(F32)
            loc_s[...] = local_state(r0, wxs[u])
            dec_s[...] = jnp.broadcast_to(decs[u], dec_s.shape)

    def stage_b(grp, slot):
        chunks = [grp * uf + u for u in range(uf)]
        ys = []
        for u, c in enumerate(chunks):
            cb_s, exf_s, yp_s, loc_s, dec_s = stage_s[slot][u]
            r0 = pl.multiple_of(c * CHUNK, CHUNK)
            ys.append(yp_s[...] + intra_chunk(x_ref[0, pl.ds(r0, CHUNK), :], cb_s[...],
                                              acs_ref[0, pl.ds(r0, CHUNK), :],
                                              dt_ref[0, pl.ds(r0, CHUNK), :]))
        hf = hf_s[...]
        for u, c in enumerate(chunks):
            cb_s, exf_s, yp_s, loc_s, dec_s = stage_s[slot][u]
            r0 = pl.multiple_of(c * CHUNK, CHUNK)
            cc = c_ref[0, pl.ds(r0, CHUNK), :]
            y = ys[u] + _dot(cc, hf.astype(BF16)) * exf_s[...].astype(F32)
            uu = y * z_ref[0, pl.ds(r0, CHUNK), :].astype(F32)
            ms_u = jnp.mean(uu * uu, axis=-1, keepdims=True)
            y_ref[0, pl.ds(r0, CHUNK), :] = (
                (uu * lax.rsqrt(ms_u + NORM_EPS)) * nw_ref[...]).astype(BF16)
            hf = hf * dec_s[0:1, :] + loc_s[...]
        hf_s[...] = hf

    def pipelined_body(j, carry):
        stage_a(2 * j + 1, 1)
        stage_b(2 * j, 0)
        stage_a(2 * j + 2, 0)
        stage_b(2 * j + 1, 1)
        return carry

    if with_output:
        assert n_grp % 2 == 0
        stage_a(0, 0)
        lax.fori_loop(0, n_grp // 2, pipelined_body, 0)
    else:
        lax.fori_loop(0, n_grp, states_only_body, 0)

    if not with_output:
        st_ref[0, 0, 0] = hf_s[...]
        st_ref[0, 0, 1] = hb_s[...]


def _ssd_call(xbc, dt, acs, h0, z, dsk, nw):
    bn, ls, _ = xbc.shape
    nc = ls // CHUNK
    with_output = z is not None
    has_h0 = h0 is not None
    gi = GROUP_INNER
    n_x = D_INNER // D_STATE
    in_specs = [
        pl.BlockSpec((1, ls, gi), lambda b, g: (b, 0, g)),
        pl.BlockSpec((1, ls, D_STATE), lambda b, g: (b, 0, n_x + g)),
        pl.BlockSpec((1, ls, D_STATE), lambda b, g: (b, 0, n_x + N_BC_GROUPS + g)),
        pl.BlockSpec((1, ls, LANES), lambda b, g: (b, 0, g)),
        pl.BlockSpec((1, ls, LANES), lambda b, g: (b, 0, g)),
    ]
    args = [xbc, xbc, xbc, dt, acs]
    st_spec = pl.BlockSpec((1, 1, N_DIR, D_STATE, gi), lambda b, g: (b, g, 0, 0, 0))
    if has_h0:
        in_specs.append(st_spec)
        args.append(h0)
    if with_output:
        in_specs += [
            pl.BlockSpec((1, ls, gi), lambda b, g: (b, 0, g)),
            pl.BlockSpec((1, gi), lambda b, g: (0, g)),
            pl.BlockSpec((1, gi), lambda b, g: (0, g)),
        ]
        args += [z, dsk, nw]
        out_specs = pl.BlockSpec((1, ls, gi), lambda b, g: (b, 0, g))
        out_shape = jax.ShapeDtypeStruct((bn, ls, D_INNER), BF16)
    else:
        out_specs = st_spec
        out_shape = jax.ShapeDtypeStruct((bn, N_BC_GROUPS, N_DIR, D_STATE, gi), F32)
    scratch = [
        pltpu.VMEM((LANES, gi), BF16),
        pltpu.VMEM((LANES, gi), BF16),
        pltpu.VMEM((D_STATE, gi), F32),
        pltpu.VMEM((D_STATE, gi), F32),
    ]
    if with_output:
        scratch.append(pltpu.VMEM((nc, D_STATE, gi), BF16))
        scratch += SSD_STAGE_SLOTS * SSD_FWD_CHUNKS * [
            pltpu.VMEM((CHUNK, CHUNK), F32),
            pltpu.VMEM((CHUNK, gi), BF16),
            pltpu.VMEM((CHUNK, gi), F32),
            pltpu.VMEM((D_STATE, gi), F32),
            pltpu.VMEM((SUBLANES, gi), F32),
        ]
    return pl.pallas_call(
        functools.partial(_ssd_kernel, nc=nc, with_output=with_output, has_h0=has_h0),
        grid=(bn, N_BC_GROUPS),
        in_specs=in_specs,
        out_specs=out_specs,
        out_shape=out_shape,
        scratch_shapes=scratch,
        compiler_params=pltpu.CompilerParams(
            dimension_semantics=("parallel", "parallel"), vmem_limit_bytes=VMEM_LIMIT_BYTES),
        name="ssd" if with_output else "ssd_ctx",
    )(*args)


POOL_TB = 4 * GRID_W
POOL_PAD_ROWS = 8
POOL_MM_ROWS = 512


def _pool_kernel(v_ref, zp_ref, pw_ref, ps_ref, o_ref, cm_s, d_s, csc_s, *, ls):
    g = pl.program_id(1)
    rows = ls // GRID_W
    pad = POOL_PAD_ROWS * GRID_W
    pg = POOL_GROUP

    def body(k):
        lo, hi = k // 2, k - 1 - k // 2
        ti = lax.broadcasted_iota(jnp.int32, (POOL_TB, POOL_TB), 0)
        si = lax.broadcasted_iota(jnp.int32, (POOL_TB, POOL_TB), 1)
        dlt = si - ti
        band = ((ti // GRID_W) == (si // GRID_W)) & (dlt >= -lo) & (dlt <= hi)
        pmat = band.astype(BF16)
        tc = lax.broadcasted_iota(jnp.int32, (POOL_TB, pg), 0) % GRID_W
        cnt_c = jnp.minimum(tc + hi, GRID_W - 1) - jnp.maximum(tc - lo, 0) + 1
        csc_s[...] = 1.0 / cnt_c.astype(F32)
        cm_s[0:pad, :] = jnp.zeros((pad, pg), F32)
        cm_s[pad + ls:2 * pad + ls, :] = jnp.zeros((pad, pg), F32)

        def col_pass(tb, carry):
            r0 = pl.multiple_of(tb * POOL_TB, POOL_TB)
            cm = _dot(pmat, v_ref[0, pl.ds(r0, POOL_TB), :]) * csc_s[...]
            cm_s[pl.ds(pad + r0, POOL_TB), :] = cm
            return carry

        lax.fori_loop(0, ls // POOL_TB, col_pass, 0, unroll=POOL_UNROLL)

        def row_pass(r, carry):
            r0 = pl.multiple_of(r * GRID_W, GRID_W)
            acc = cm_s[pl.ds(pad + r0 - lo * GRID_W, GRID_W), :]
            for o in range(-lo + 1, hi + 1):
                acc = acc + cm_s[pl.ds(pad + r0 + o * GRID_W, GRID_W), :]
            cnt_r = jnp.minimum(r + hi, rows - 1) - jnp.maximum(r - lo, 0) + 1
            inv_r = 1.0 / jnp.full((1, pg), cnt_r, jnp.int32).astype(F32)
            m = acc * inv_r
            d = m - v_ref[0, pl.ds(r0, GRID_W), :].astype(F32)
            d_s[pl.ds(r0, GRID_W), :] = d.astype(BF16)
            return carry

        lax.fori_loop(0, rows, row_pass, 0, unroll=POOL_UNROLL)

        def mm(tb, carry):
            r0 = pl.multiple_of(tb * POOL_MM_ROWS, POOL_MM_ROWS)
            y = _dot(d_s[pl.ds(r0, POOL_MM_ROWS), :], pw_ref[0])
            y = (y * ps_ref[...]) * zp_ref[0, pl.ds(r0, POOL_MM_ROWS), :].astype(F32)
            o_ref[0, pl.ds(r0, POOL_MM_ROWS), :] = y.astype(BF16)
            return carry

        lax.fori_loop(0, ls // POOL_MM_ROWS, mm, 0, unroll=POOL_UNROLL)

    for gidx, k in enumerate(POOL_WINDOWS):
        pl.when(g == gidx)(functools.partial(body, k))


def _pool_call(v, zp, pool_w, pool_scale):
    bn, ls, _ = v.shape
    pg = POOL_GROUP
    tok = pl.BlockSpec((1, ls, pg), lambda b, g: (b, 0, g))
    return pl.pallas_call(
        functools.partial(_pool_kernel, ls=ls),
        grid=(bn, N_POOL_GROUPS),
        in_specs=[
            tok,
            tok,
            pl.BlockSpec((1, pg, pg), lambda b, g: (g, 0, 0)),
            pl.BlockSpec((1, pg), lambda b, g: (0, g)),
        ],
        out_specs=tok,
        out_shape=jax.ShapeDtypeStruct((bn, ls, D_MODEL), BF16),
        scratch_shapes=[
            pltpu.VMEM((ls + 2 * POOL_PAD_ROWS * GRID_W, pg), F32),
            pltpu.VMEM((ls, pg), BF16),
            pltpu.VMEM((POOL_TB, pg), F32),
        ],
        compiler_params=pltpu.CompilerParams(
            dimension_semantics=("parallel", "parallel"), vmem_limit_bytes=VMEM_LIMIT_BYTES),
        name="pool",
    )(v, zp, pool_w, pool_scale)


def _out_kernel(yp_ref, ys_ref, gt_ref, x_ref, gate_ref, wpost_ref, wpp_ref, wps_ref, wo_ref, o_ref):
    a = _dot(yp_ref[0], wpp_ref[...])
    b = _dot(ys_ref[0], wps_ref[...])
    merged = (gt_ref[0, :, 0:D_MODEL].astype(F32) * a
              + gt_ref[0, :, D_MODEL:2 * D_MODEL].astype(F32) * b)
    o = _dot(merged.astype(BF16), wo_ref[...])
    ms = jnp.mean(o * o, axis=-1, keepdims=True)
    on = (o * lax.rsqrt(ms + NORM_EPS)) * wpost_ref[...]
    o_ref[0] = x_ref[0] + gate_ref[0] * on


def _out_call(yp, ys, gates, x, mod3, wpost, wpp, wps, wo):
    bn, ls, _ = x.shape
    tm = TM_OUT
    tok = lambda width: pl.BlockSpec((1, tm, width), lambda b, i: (b, i, 0))
    return pl.pallas_call(
        _out_kernel,
        grid=(bn, ls // tm),
        in_specs=[
            tok(D_MODEL), tok(D_INNER), tok(2 * D_MODEL), tok(D_MODEL),
            pl.BlockSpec((1, 1, D_MODEL), lambda b, i: (b, 0, 2)),
            _resident(wpost.shape), _resident(wpp.shape), _resident(wps.shape), _resident(wo.shape),
        ],
        out_specs=tok(D_MODEL),
        out_shape=jax.ShapeDtypeStruct(x.shape, F32),
        compiler_params=pltpu.CompilerParams(
            dimension_semantics=("parallel", "parallel"), vmem_limit_bytes=VMEM_LIMIT_BYTES),
        name="out",
    )(yp, ys, gates, x, mod3, wpost, wpp, wps, wo)


def _dt_layout_index():
    idx = []
    zero_col = N_DIR * N_HEADS
    for g in range(N_BC_GROUPS):
        blk = [g * HEADS_PER_GROUP + r for r in range(HEADS_PER_GROUP)]
        blk += [N_HEADS + g * HEADS_PER_GROUP + r for r in range(HEADS_PER_GROUP)]
        blk += [zero_col] * (LANES - len(blk))
        idx += blk
    return jnp.asarray(idx, jnp.int32)


def _dt_layout(cols):
    padded = jnp.concatenate([cols, jnp.zeros((cols.shape[0], 1), cols.dtype)], axis=1)
    return jnp.take(padded, _dt_layout_index(), axis=1)


def kernel(x, c, ctx, c_ctx, w_ada, b_ada, norm_pre, norm_post, w_in, b_merge, pool_w, pool_scale,
           conv_w, conv_b, dt_bias, a_log, d_skip, ssd_norm, w_proj_pool, w_proj_ssd, w_out):
    assert w_ada.shape[0] == 1, "single-layer block"
    bn = x.shape[0]
    assert x.shape[1] % (GRID_W * GRID_W) == 0 and x.shape[1] // GRID_W == GRID_W

    n_cond = bn + 1
    cond_rows = -(-n_cond // SUBLANES) * SUBLANES
    cc = jnp.concatenate([c, c_ctx[None, :], jnp.zeros((cond_rows - n_cond, D_MODEL), F32)], axis=0)
    mod = _mod_call(cc, w_ada[0], b_ada[0][None, :])
    mod3 = mod.reshape(cond_rows, 1, 3 * D_MODEL)

    w_in0 = w_in[0]
    w_main = w_in0[:, :OFF_DT].astype(BF16)
    w_dt = _dt_layout(w_in0[:, OFF_DT:]).astype(BF16)
    dtb_lay = _dt_layout(dt_bias[0].reshape(1, N_DIR * N_HEADS))
    alog_lay = _dt_layout(a_log[0].reshape(1, N_DIR * N_HEADS))
    wpre = norm_pre[0][None, :]
    wpost = norm_post[0][None, :]
    cw = conv_w[0]
    cb = conv_b[0][None, :]
    dsk = jnp.repeat(d_skip[0], HEAD_DIM)[None, :]
    nw = ssd_norm[0][None, :]

    xbc_c, dt_c, acs_c = _inproj_call(ctx, mod3, bn, wpre, w_main, w_dt, None, cw, cb,
                                      dtb_lay, alog_lay, ctx=True)
    h0 = _ssd_call(xbc_c, dt_c, acs_c, None, None, None, None)

    v, zp, zs, gates, xbc, dt, acs = _inproj_call(
        x, mod3, None, wpre, w_main, w_dt, b_merge[0][None, :], cw, cb, dtb_lay, alog_lay,
        ctx=False)
    y_pool = _pool_call(v, zp, pool_w[0].astype(BF16), pool_scale[0][None, :])
    y_ssd = _ssd_call(xbc, dt, acs, h0, zs, dsk, nw)
    return _out_call(y_pool, y_ssd, gates, x, mod3, wpost,
                     w_proj_pool[0].astype(BF16), w_proj_ssd[0].astype(BF16), w_out[0].astype(BF16))
```

```python
import functools

import jax
import jax.numpy as jnp
from jax import lax
from jax.experimental import pallas as pl
from jax.experimental.pallas import tpu as pltpu

F32 = jnp.float32
BF16 = jnp.bfloat16

D_MODEL = 1024
GRID_W = 64
NORM_EPS = 1e-6
LOG2_E = 1.4426950408889634
POOL_WINDOWS = (2, 4, 8, 16)
N_POOL_GROUPS = 4
POOL_GROUP = D_MODEL // N_POOL_GROUPS
D_INNER = 2 * D_MODEL
HEAD_DIM = 64
N_HEADS = D_INNER // HEAD_DIM
D_STATE = 128
N_BC_GROUPS = 4
HEADS_PER_GROUP = N_HEADS // N_BC_GROUPS
GROUP_INNER = D_INNER // N_BC_GROUPS
CONV_K = 4
CONV_LEFT = CONV_K // 2
CHUNK = 128
N_DIR = 2
CONV_DIM = D_INNER + 2 * N_BC_GROUPS * D_STATE
OFF_POOL_V = 0
OFF_POOL_Z = OFF_POOL_V + D_MODEL
OFF_SSD_Z = OFF_POOL_Z + D_MODEL
OFF_GATE = OFF_SSD_Z + D_INNER
OFF_XBC = OFF_GATE + 2 * D_MODEL
OFF_DT = OFF_XBC + CONV_DIM
IN_COLS = OFF_DT + N_DIR * N_HEADS

LANES = 128
SUBLANES = 8
BF16_ROWS = 16
VMEM_LIMIT_BYTES = 60 * 1024 * 1024

DT_COLS = N_BC_GROUPS * LANES

TM_PROJ = 512
TN_PROJ = 256
TM_OUT = 512
SSD_BWD_CHUNKS = 4
SSD_FWD_CHUNKS = 2
SSD_STAGE_SLOTS = 2
SSD_STAGE_BUFS = 5
POOL_UNROLL = 4
CONV_HALO = BF16_ROWS
CONV_STRIDE = 4
CONV_BLK = CONV_STRIDE * SUBLANES
CONV_SLOTS = 2


def _sigmoid(x):
    return 1.0 / (1.0 + jnp.exp2(x * (-LOG2_E)))


def _silu(x):
    return x * _sigmoid(x)


def _softplus(x):
    return jnp.maximum(x, 0.0) + jnp.log1p(jnp.exp(-jnp.abs(x)))


def _split3(a):
    hi = a.astype(BF16)
    r1 = a - hi.astype(F32)
    mid = r1.astype(BF16)
    lo = (r1 - mid.astype(F32)).astype(BF16)
    return hi, mid, lo


def _dot(a, b):
    return jnp.dot(a, b, preferred_element_type=F32)


def _dot3(a_f32, b_bf16):
    hi, mid, lo = _split3(a_f32)
    return _dot(hi, b_bf16) + _dot(mid, b_bf16) + _dot(lo, b_bf16)


def _mod_kernel(c_ref, w_ref, b_ref, o_ref):
    s = _silu(c_ref[...])
    o_ref[...] = _dot(s, w_ref[...]) + b_ref[...]


def _mod_call(cc, w_ada, b_ada):
    rows = cc.shape[0]
    tn = 512
    return pl.pallas_call(
        _mod_kernel,
        grid=(3 * D_MODEL // tn,),
        in_specs=[
            pl.BlockSpec((rows, D_MODEL), lambda j: (0, 0)),
            pl.BlockSpec((D_MODEL, tn), lambda j: (0, j)),
            pl.BlockSpec((1, tn), lambda j: (0, j)),
        ],
        out_specs=pl.BlockSpec((rows, tn), lambda j: (0, j)),
        out_shape=jax.ShapeDtypeStruct((rows, 3 * D_MODEL), F32),
        name="mod",
    )(cc, w_ada, b_ada)


def _norm_modulate(x, wpre, scale, shift):
    ms = jnp.mean(x * x, axis=-1, keepdims=True)
    xn = x * lax.rsqrt(ms + NORM_EPS)
    return (xn * wpre) * (1.0 + scale) + shift


def _dt_epilogue(acc, dtb, alog, dt_ref, acs_ref, tm):
    lane = lax.broadcasted_iota(jnp.int32, (1, DT_COLS), 1) & (LANES - 1)
    used = lane < 2 * HEADS_PER_GROUP
    is_bwd = used & (lane >= HEADS_PER_GROUP)
    dt = _softplus(acc + dtb)
    a = dt * jnp.where(used, -LOG2_E * jnp.exp(alog), 0.0)
    dt_ref[0] = dt
    ri = lax.broadcasted_iota(jnp.int32, (CHUNK, CHUNK), 0)
    ci = lax.broadcasted_iota(jnp.int32, (CHUNK, CHUNK), 1)
    tri = (ri >= ci).astype(BF16)
    for q in range(tm // CHUNK):
        aq = a[q * CHUNK:(q + 1) * CHUNK]
        fwd = _dot3_lhs01(tri, aq)
        tot = fwd[CHUNK - 1:CHUNK]
        bwd = tot - fwd + aq
        acs_ref[0, q * CHUNK:(q + 1) * CHUNK, :] = jnp.where(is_bwd, bwd, fwd)


def _dot3_lhs01(m01_bf16, a_f32):
    hi, mid, lo = _split3(a_f32)
    return _dot(m01_bf16, hi) + _dot(m01_bf16, mid) + _dot(m01_bf16, lo)


def _conv_silu_segment(raw_s, cs_s, cw_ref, cb_ref, col0, tm):
    n_ld = CONV_STRIDE + CONV_K - 1
    for lc in range(TN_PROJ // LANES):
        pcols = slice(col0 + lc * LANES, col0 + (lc + 1) * LANES)
        wk = [jnp.broadcast_to(cw_ref[k:k + 1, pcols], (SUBLANES, LANES)) for k in range(CONV_K)]
        bb = jnp.broadcast_to(cb_ref[:, pcols], (SUBLANES, LANES))
        for blk in range(tm // CONV_BLK):
            base = CONV_HALO + blk * CONV_BLK - CONV_LEFT
            lds = [raw_s[lc][base + o:base + o + CONV_BLK:CONV_STRIDE, :] for o in range(n_ld)]
            for j in range(CONV_STRIDE):
                acc = lds[j] * wk[0]
                for k in range(1, CONV_K):
                    acc = acc + lds[j + k] * wk[k]
                acc = acc + bb
                cs_s[lc][blk * CONV_BLK + j:(blk + 1) * CONV_BLK:CONV_STRIDE, :] = _silu(acc)


def _stage_rows(x_ref, xp_ref, xn_ref, sh_ref, sc_ref, wpre_ref, hb_s, tm):
    i = pl.program_id(1)
    mp = (i > 0).astype(F32)
    mn = (i < pl.num_programs(1) - 1).astype(F32)
    wpre, sc, sh = wpre_ref[...], sc_ref[0], sh_ref[0]
    hb_s[0:CONV_HALO, :] = (_norm_modulate(xp_ref[0], wpre, sc, sh) * mp).astype(BF16)
    hb_s[CONV_HALO:CONV_HALO + tm, :] = _norm_modulate(x_ref[0], wpre, sc, sh).astype(BF16)
    hb_s[CONV_HALO + tm:2 * CONV_HALO + tm, :] = (
        _norm_modulate(xn_ref[0], wpre, sc, sh) * mn).astype(BF16)


def _slot_buffers(conv_scratch, k):
    n_lc = TN_PROJ // LANES
    slot = conv_scratch[(k % CONV_SLOTS) * 2 * n_lc:(k % CONV_SLOTS + 1) * 2 * n_lc]
    return slot[:n_lc], slot[n_lc:]


def _run_segments(segments, matmul_fn, epilogue_fn):
    pending = []
    for k, seg in enumerate(segments):
        matmul_fn(k, seg)
        pending.append((k, seg))
        if len(pending) == CONV_SLOTS:
            epilogue_fn(*pending.pop(0))
    for k, seg in pending:
        epilogue_fn(k, seg)


def _segment_fns(hb_s, w_ref, bm_ref, cw_ref, cb_ref, outs, conv_scratch, tm):
    v_ref, zp_ref, zs_ref, gt_ref, xbc_ref = outs
    n_lc = TN_PROJ // LANES
    rows = slice(CONV_HALO, CONV_HALO + tm)

    def matmul_fn(k, seg):
        kind, col = seg
        raw_s, _ = _slot_buffers(conv_scratch, k)
        if kind == "conv":
            acc = _dot(hb_s[...], w_ref[:, OFF_XBC + col:OFF_XBC + col + TN_PROJ])
            for lc in range(n_lc):
                raw_s[lc][...] = acc[:, lc * LANES:(lc + 1) * LANES]
        else:
            acc = _dot(hb_s[rows, :], w_ref[:, col:col + TN_PROJ])
            for lc in range(n_lc):
                raw_s[lc][rows, :] = acc[:, lc * LANES:(lc + 1) * LANES]

    def epilogue_fn(k, seg):
        kind, col = seg
        raw_s, cs_s = _slot_buffers(conv_scratch, k)
        if kind == "conv":
            _conv_silu_segment(raw_s, cs_s, cw_ref, cb_ref, col, tm)
            for lc in range(n_lc):
                xbc_ref[0, :, col + lc * LANES:col + (lc + 1) * LANES] = cs_s[lc][...].astype(BF16)
            return
        for lc in range(n_lc):
            acc = raw_s[lc][rows, :]
            c0 = col + lc * LANES
            if c0 < OFF_POOL_Z:
                v_ref[0, :, c0 - OFF_POOL_V:c0 - OFF_POOL_V + LANES] = acc.astype(BF16)
            elif c0 < OFF_SSD_Z:
                zp_ref[0, :, c0 - OFF_POOL_Z:c0 - OFF_POOL_Z + LANES] = _silu(acc).astype(BF16)
            elif c0 < OFF_GATE:
                zs_ref[0, :, c0 - OFF_SSD_Z:c0 - OFF_SSD_Z + LANES] = _silu(acc).astype(BF16)
            else:
                o = c0 - OFF_GATE
                gt_ref[0, :, o:o + LANES] = _sigmoid(acc + bm_ref[:, o:o + LANES]).astype(BF16)

    return matmul_fn, epilogue_fn


def _inproj_main_kernel(x_ref, xp_ref, xn_ref, sh_ref, sc_ref, wpre_ref, w_ref, wdt_ref, bm_ref,
                        cw_ref, cb_ref, dtb_ref, alog_ref,
                        v_ref, zp_ref, zs_ref, gt_ref, xbc_ref, dt_ref, acs_ref,
                        hb_s, *conv_scratch, tm):
    _stage_rows(x_ref, xp_ref, xn_ref, sh_ref, sc_ref, wpre_ref, hb_s, tm)
    n_plain = OFF_XBC // TN_PROJ
    n_conv = CONV_DIM // TN_PROJ
    per_conv = n_plain // n_conv
    segments = []
    for q in range(n_conv):
        segments += [("plain", s * TN_PROJ) for s in range(q * per_conv, (q + 1) * per_conv)]
        segments.append(("conv", q * TN_PROJ))
    segments += [("plain", s * TN_PROJ) for s in range(n_conv * per_conv, n_plain)]
    _run_segments(segments, *_segment_fns(hb_s, w_ref, bm_ref, cw_ref, cb_ref,
                                          (v_ref, zp_ref, zs_ref, gt_ref, xbc_ref),
                                          conv_scratch, tm))
    acc = _dot(hb_s[CONV_HALO:CONV_HALO + tm, :], wdt_ref[...])
    _dt_epilogue(acc, dtb_ref[...], alog_ref[...], dt_ref, acs_ref, tm)


def _inproj_ctx_kernel(x_ref, xp_ref, xn_ref, sh_ref, sc_ref, wpre_ref, w_ref, wdt_ref,
                       cw_ref, cb_ref, dtb_ref, alog_ref,
                       xbc_ref, dt_ref, acs_ref, hb_s, *conv_scratch, tm):
    _stage_rows(x_ref, xp_ref, xn_ref, sh_ref, sc_ref, wpre_ref, hb_s, tm)
    segments = [("conv", s * TN_PROJ) for s in range(CONV_DIM // TN_PROJ)]
    _run_segments(segments, *_segment_fns(hb_s, w_ref, None, cw_ref, cb_ref,
                                          (None, None, None, None, xbc_ref), conv_scratch, tm))
    acc = _dot(hb_s[CONV_HALO:CONV_HALO + tm, :], wdt_ref[...])
    _dt_epilogue(acc, dtb_ref[...], alog_ref[...], dt_ref, acs_ref, tm)


def _resident(shape):
    nd = len(shape)
    return pl.BlockSpec(shape, lambda *_: (0,) * nd, pipeline_mode=pl.Buffered(1))


def _inproj_call(xin, mod3, mod_row, wpre, w_main, w_dt, b_merge, conv_w, conv_b, dtb_lay, alog_lay,
                 *, ctx):
    bn, ln, _ = xin.shape
    tm = min(TM_PROJ, ln)
    grid = (bn, ln // tm)
    halo_per_blk = tm // CONV_HALO
    n_halo_blk = ln // CONV_HALO
    if mod_row is None:
        row = lambda b, i: b
    else:
        row = lambda b, i: mod_row
    tok = lambda width: pl.BlockSpec((1, tm, width), lambda b, i: (b, i, 0))
    in_specs = [
        tok(D_MODEL),
        pl.BlockSpec((1, CONV_HALO, D_MODEL),
                     lambda b, i: (b, jnp.maximum(i * halo_per_blk - 1, 0), 0)),
        pl.BlockSpec((1, CONV_HALO, D_MODEL),
                     lambda b, i: (b, jnp.minimum((i + 1) * halo_per_blk, n_halo_blk - 1), 0)),
        pl.BlockSpec((1, 1, D_MODEL), lambda b, i: (row(b, i), 0, 0)),
        pl.BlockSpec((1, 1, D_MODEL), lambda b, i: (row(b, i), 0, 1)),
        _resident((1, D_MODEL)),
        _resident(w_main.shape),
        _resident(w_dt.shape),
    ]
    args = [xin, xin, xin, mod3, mod3, wpre, w_main, w_dt]
    if not ctx:
        in_specs.append(_resident(b_merge.shape))
        args.append(b_merge)
    in_specs += [_resident(conv_w.shape), _resident(conv_b.shape),
                 _resident(dtb_lay.shape), _resident(alog_lay.shape)]
    args += [conv_w, conv_b, dtb_lay, alog_lay]
    tail_specs = [tok(CONV_DIM), tok(DT_COLS), tok(DT_COLS)]
    tail_shapes = [
        jax.ShapeDtypeStruct((bn, ln, CONV_DIM), BF16),
        jax.ShapeDtypeStruct((bn, ln, DT_COLS), F32),
        jax.ShapeDtypeStruct((bn, ln, DT_COLS), F32),
    ]
    if ctx:
        body = functools.partial(_inproj_ctx_kernel, tm=tm)
        out_specs, out_shape = tail_specs, tail_shapes
    else:
        body = functools.partial(_inproj_main_kernel, tm=tm)
        out_specs = [tok(D_MODEL), tok(D_MODEL), tok(D_INNER), tok(2 * D_MODEL)] + tail_specs
        out_shape = [
            jax.ShapeDtypeStruct((bn, ln, D_MODEL), BF16),
            jax.ShapeDtypeStruct((bn, ln, D_MODEL), BF16),
            jax.ShapeDtypeStruct((bn, ln, D_INNER), BF16),
            jax.ShapeDtypeStruct((bn, ln, 2 * D_MODEL), BF16),
        ] + tail_shapes
    return pl.pallas_call(
        body,
        grid=grid,
        in_specs=in_specs,
        out_specs=out_specs,
        out_shape=out_shape,
        scratch_shapes=[pltpu.VMEM((tm + 2 * CONV_HALO, D_MODEL), BF16)] + CONV_SLOTS * (
            (TN_PROJ // LANES) * [pltpu.VMEM((tm + 2 * CONV_HALO, LANES), F32)]
            + (TN_PROJ // LANES) * [pltpu.VMEM((tm, LANES), F32)]),
        compiler_params=pltpu.CompilerParams(
            dimension_semantics=("parallel", "parallel"), vmem_limit_bytes=VMEM_LIMIT_BYTES),
        name="inproj_ctx" if ctx else "inproj",
    )(*args)


def _ssd_kernel(*refs, nc, with_output, has_h0):
    it = iter(refs)
    x_ref, b_ref, c_ref, dt_ref, acs_ref = (next(it) for _ in range(5))
    h0_ref = next(it) if has_h0 else None
    if with_output:
        z_ref, dsk_ref, nw_ref = (next(it) for _ in range(3))
        y_ref = next(it)
    else:
        st_ref = next(it)
    ef_s, eb_s, hf_s, hb_s = (next(it) for _ in range(4))
    hbin_s = next(it) if with_output else None
    if with_output:
        stage_s = [[tuple(next(it) for _ in range(SSD_STAGE_BUFS)) for _ in range(SSD_FWD_CHUNKS)]
                   for _ in range(SSD_STAGE_SLOTS)]

    gi = GROUP_INNER
    hpg = HEADS_PER_GROUP

    er = lax.broadcasted_iota(jnp.int32, (LANES, gi), 0)
    el = lax.broadcasted_iota(jnp.int32, (LANES, gi), 1) // HEAD_DIM
    ef_s[...] = (er == el).astype(BF16)
    eb_s[...] = (er == el + hpg).astype(BF16)

    lane = lax.broadcasted_iota(jnp.int32, (1, LANES), 1)
    is_bwd = (lane >= hpg) & (lane < 2 * hpg)

    def chunk_state_terms(c):
        r0 = pl.multiple_of(c * CHUNK, CHUNK)
        dtc = dt_ref[0, pl.ds(r0, CHUNK), :]
        acs = acs_ref[0, pl.ds(r0, CHUNK), :]
        tot = jnp.where(is_bwd, acs[0:1, :], acs[CHUNK - 1:CHUNK, :])
        w = dtc * jnp.exp2(tot - acs)
        return r0, dtc, acs, tot, w

    def expand(parts, e_s):
        full = _dot(jnp.concatenate([p.astype(BF16) for p in parts], axis=0), e_s[...])
        return [full[k * CHUNK:(k + 1) * CHUNK] for k in range(len(parts))]

    def local_state(r0, wx):
        xw = x_ref[0, pl.ds(r0, CHUNK), :] * wx.astype(BF16)
        bc = b_ref[0, pl.ds(r0, CHUNK), :]
        return lax.dot_general(bc, xw, (((0,), (0,)), ((), ())), preferred_element_type=F32)

    def chunk_decays(tots, e_s):
        t8 = jnp.concatenate([jnp.broadcast_to(t, (SUBLANES, LANES)) for t in tots], axis=0)
        d = jnp.exp2(_dot3(t8, e_s[...]))
        return [d[k * SUBLANES:k * SUBLANES + 1, :] for k in range(len(tots))]

    if has_h0:
        hf_s[...] = h0_ref[0, 0, 0]
        hb_s[...] = h0_ref[0, 0, 1]
    else:
        hf_s[...] = jnp.zeros_like(hf_s)
        hb_s[...] = jnp.zeros_like(hb_s)


    ub = min(SSD_BWD_CHUNKS, nc)

    def bwd_body(i, carry):
        chunks = [nc - 1 - (i * ub + u) for u in range(ub)]
        terms = [chunk_state_terms(c) for c in chunks]
        wxs = expand([t[4] for t in terms], eb_s)
        decs = chunk_decays([t[3] for t in terms], eb_s)
        locs = [local_state(t[0], wx) for t, wx in zip(terms, wxs)]
        hb = hb_s[...]
        for c, s_loc, dec in zip(chunks, locs, decs):
            if with_output:
                hbin_s[c] = hb.astype(BF16)
            hb = hb * dec + s_loc
        hb_s[...] = hb
        return carry

    lax.fori_loop(0, nc // ub, bwd_body, 0)

    ri = lax.broadcasted_iota(jnp.int32, (CHUNK, CHUNK), 0)
    ci = lax.broadcasted_iota(jnp.int32, (CHUNK, CHUNK), 1)
    lower = ri > ci
    eye_b = (ri == ci).astype(BF16)
    pair_lane = lax.broadcasted_iota(jnp.int32, (CHUNK, 2 * HEAD_DIM), 1)
    first_head = pair_lane < HEAD_DIM

    def intra_chunk(xc, cb, acs, dtc):
        lrow_t = (acs - jnp.log2(dtc)).T
        dt_t = dtc.T
        cbb = cb.astype(BF16)
        pieces = []
        for p in range(hpg // 2):
            ms = []
            for r in (2 * p, 2 * p + 1):
                colf = acs[:, r:r + 1]
                colb = acs[:, hpg + r:hpg + r + 1]
                rowf = lrow_t[r:r + 1, :]
                rowb = lrow_t[hpg + r:hpg + r + 1, :]
                arg = jnp.where(lower, colf, colb) - jnp.where(lower, rowf, rowb)
                dtf = dt_t[r:r + 1, :].astype(BF16)
                ms.append(cbb * (jnp.exp2(arg).astype(BF16) + eye_b * dtf))
            xp = xc[:, p * 2 * HEAD_DIM:(p + 1) * 2 * HEAD_DIM]
            zero = jnp.zeros_like(xp)
            rhs = jnp.concatenate([jnp.where(first_head, xp, zero),
                                   jnp.where(first_head, zero, xp)], axis=0)
            pieces.append(_dot(jnp.concatenate(ms, axis=1), rhs))
        return jnp.concatenate(pieces, axis=1)

    uf = min(SSD_FWD_CHUNKS, nc)

    n_grp = nc // uf

    def states_only_body(i, carry):
        chunks = [i * uf + u for u in range(uf)]
        terms = [chunk_state_terms(c) for c in chunks]
        decs = chunk_decays([t[3] for t in terms], ef_s)
        wxs = expand([t[4] for t in terms], ef_s)
        locs = [local_state(t[0], wx) for t, wx in zip(terms, wxs)]
        hf = hf_s[...]
        for u in range(uf):
            hf = hf * decs[u] + locs[u]
        hf_s[...] = hf
        return carry

    def stage_a(grp, slot):
        grp = jnp.minimum(grp, n_grp - 1)
        chunks = [grp * uf + u for u in range(uf)]
        terms = [chunk_state_terms(c) for c in chunks]
        decs = chunk_decays([t[3] for t in terms], ef_s)
        eacs = [jnp.exp2(t[2]) for t in terms]
        ex_f = expand(eacs + [t[4] for t in terms], ef_s)
        exfs, wxs = ex_f[:uf], ex_f[uf:]
        exbs = expand(eacs, eb_s)
        for u, (c, (r0, dtc, acs, tot, w)) in enumerate(zip(chunks, terms)):
            cb_s, exf_s, yp_s, loc_s, dec_s = stage_s[slot][u]
            xc = x_ref[0, pl.ds(r0, CHUNK), :]
            cc = c_ref[0, pl.ds(r0, CHUNK), :]
            cb_s[...] = lax.dot_general(cc, b_ref[0, pl.ds(r0, CHUNK), :],
                                        (((1,), (1,)), ((), ())), preferred_element_type=F32)
            exf_s[...] = exfs[u].astype(BF16)
            yp_s[...] = _dot(cc, hbin_s[c]) * exbs[u] + dsk_ref[...] * xc.astype(F32)
            loc_s[...] = local_state(r0, wxs[u])
            dec_s[...] = jnp.broadcast_to(decs[u], dec_s.shape)

    def stage_b(grp, slot):
        chunks = [grp * uf + u for u in range(uf)]
        ys = []
        for u, c in enumerate(chunks):
            cb_s, exf_s, yp_s, loc_s, dec_s = stage_s[slot][u]
            r0 = pl.multiple_of(c * CHUNK, CHUNK)
            ys.append(yp_s[...] + intra_chunk(x_ref[0, pl.ds(r0, CHUNK), :], cb_s[...],
                                              acs_ref[0, pl.ds(r0, CHUNK), :],
                                              dt_ref[0, pl.ds(r0, CHUNK), :]))
        hf = hf_s[...]
        for u, c in enumerate(chunks):
            cb_s, exf_s, yp_s, loc_s, dec_s = stage_s[slot][u]
            r0 = pl.multiple_of(c * CHUNK, CHUNK)
            cc = c_ref[0, pl.ds(r0, CHUNK), :]
            y = ys[u] + _dot(cc, hf.astype(BF16)) * exf_s[...].astype(F32)
            uu = y * z_ref[0, pl.ds(r0, CHUNK), :].astype(F32)
            ms_u = jnp.mean(uu * uu, axis=-1, keepdims=True)
            y_ref[0, pl.ds(r0, CHUNK), :] = (
                (uu * lax.rsqrt(ms_u + NORM_EPS)) * nw_ref[...]).astype(BF16)
            hf = hf * dec_s[0:1, :] + loc_s[...]
        hf_s[...] = hf

    def pipelined_body(j, carry):
        stage_a(2 * j + 1, 1)
        stage_b(2 * j, 0)
        stage_a(2 * j + 2, 0)
        stage_b(2 * j + 1, 1)
        return carry

    if with_output:
        assert n_grp % 2 == 0
        stage_a(0, 0)
        lax.fori_loop(0, n_grp // 2, pipelined_body, 0)
    else:
        lax.fori_loop(0, n_grp, states_only_body, 0)

    if not with_output:
        st_ref[0, 0, 0] = hf_s[...]
        st_ref[0, 0, 1] = hb_s[...]


def _ssd_call(xbc, dt, acs, h0, z, dsk, nw):
    bn, ls, _ = xbc.shape
    nc = ls // CHUNK
    with_output = z is not None
    has_h0 = h0 is not None
    gi = GROUP_INNER
    n_x = D_INNER // D_STATE
    in_specs = [
        pl.BlockSpec((1, ls, gi), lambda b, g: (b, 0, g)),
        pl.BlockSpec((1, ls, D_STATE), lambda b, g: (b, 0, n_x + g)),
        pl.BlockSpec((1, ls, D_STATE), lambda b, g: (b, 0, n_x + N_BC_GROUPS + g)),
        pl.BlockSpec((1, ls, LANES), lambda b, g: (b, 0, g)),
        pl.BlockSpec((1, ls, LANES), lambda b, g: (b, 0, g)),
    ]
    args = [xbc, xbc, xbc, dt, acs]
    st_spec = pl.BlockSpec((1, 1, N_DIR, D_STATE, gi), lambda b, g: (b, g, 0, 0, 0))
    if has_h0:
        in_specs.append(st_spec)
        args.append(h0)
    if with_output:
        in_specs += [
            pl.BlockSpec((1, ls, gi), lambda b, g: (b, 0, g)),
            pl.BlockSpec((1, gi), lambda b, g: (0, g)),
            pl.BlockSpec((1, gi), lambda b, g: (0, g)),
        ]
        args += [z, dsk, nw]
        out_specs = pl.BlockSpec((1, ls, gi), lambda b, g: (b, 0, g))
        out_shape = jax.ShapeDtypeStruct((bn, ls, D_INNER), BF16)
    else:
        out_specs = st_spec
        out_shape = jax.ShapeDtypeStruct((bn, N_BC_GROUPS, N_DIR, D_STATE, gi), F32)
    scratch = [
        pltpu.VMEM((LANES, gi), BF16),
        pltpu.VMEM((LANES, gi), BF16),
        pltpu.VMEM((D_STATE, gi), F32),
        pltpu.VMEM((D_STATE, gi), F32),
    ]
    if with_output:
        scratch.append(pltpu.VMEM((nc, D_STATE, gi), BF16))
        scratch += SSD_STAGE_SLOTS * SSD_FWD_CHUNKS * [
            pltpu.VMEM((CHUNK, CHUNK), F32),
            pltpu.VMEM((CHUNK, gi), BF16),
            pltpu.VMEM((CHUNK, gi), F32),
            pltpu.VMEM((D_STATE, gi), F32),
            pltpu.VMEM((SUBLANES, gi), F32),
        ]
    return pl.pallas_call(
        functools.partial(_ssd_kernel, nc=nc, with_output=with_output, has_h0=has_h0),
        grid=(bn, N_BC_GROUPS),
        in_specs=in_specs,
        out_specs=out_specs,
        out_shape=out_shape,
        scratch_shapes=scratch,
        compiler_params=pltpu.CompilerParams(
            dimension_semantics=("parallel", "parallel"), vmem_limit_bytes=VMEM_LIMIT_BYTES),
        name="ssd" if with_output else "ssd_ctx",
    )(*args)


POOL_TB = 4 * GRID_W
POOL_PAD_ROWS = 8
POOL_MM_ROWS = 512
POOL_SLIDING_MIN_WINDOW = 4


def _pool_kernel(v_ref, zp_ref, pw_ref, ps_ref, o_ref, cm_s, d_s, csc_s, *, ls):
    g = pl.program_id(1)
    rows = ls // GRID_W
    pad = POOL_PAD_ROWS * GRID_W
    pg = POOL_GROUP

    def body(k):
        lo, hi = k // 2, k - 1 - k // 2
        ti = lax.broadcasted_iota(jnp.int32, (POOL_TB, POOL_TB), 0)
        si = lax.broadcasted_iota(jnp.int32, (POOL_TB, POOL_TB), 1)
        dlt = si - ti
        band = ((ti // GRID_W) == (si // GRID_W)) & (dlt >= -lo) & (dlt <= hi)
        pmat = band.astype(BF16)
        tc = lax.broadcasted_iota(jnp.int32, (POOL_TB, pg), 0) % GRID_W
        cnt_c = jnp.minimum(tc + hi, GRID_W - 1) - jnp.maximum(tc - lo, 0) + 1
        csc_s[...] = 1.0 / cnt_c.astype(F32)
        cm_s[0:pad, :] = jnp.zeros((pad, pg), F32)
        cm_s[pad + ls:2 * pad + ls, :] = jnp.zeros((pad, pg), F32)

        def col_pass(tb, carry):
            r0 = pl.multiple_of(tb * POOL_TB, POOL_TB)
            cm = _dot(pmat, v_ref[0, pl.ds(r0, POOL_TB), :]) * csc_s[...]
            cm_s[pl.ds(pad + r0, POOL_TB), :] = cm
            return carry

        lax.fori_loop(0, ls // POOL_TB, col_pass, 0, unroll=POOL_UNROLL)

        def window_sum(r0, first, last):
            acc = cm_s[pl.ds(pad + r0 + first * GRID_W, GRID_W), :]
            for o in range(first + 1, last + 1):
                acc = acc + cm_s[pl.ds(pad + r0 + o * GRID_W, GRID_W), :]
            return acc

        def finish_row(r, r0, acc):
            cnt_r = jnp.minimum(r + hi, rows - 1) - jnp.maximum(r - lo, 0) + 1
            inv_r = 1.0 / jnp.full((1, pg), cnt_r, jnp.int32).astype(F32)
            d = acc * inv_r - v_ref[0, pl.ds(r0, GRID_W), :].astype(F32)
            d_s[pl.ds(r0, GRID_W), :] = d.astype(BF16)

        def row_pass(r, carry):
            r0 = pl.multiple_of(r * GRID_W, GRID_W)
            finish_row(r, r0, window_sum(r0, -lo, hi))
            return carry

        def row_pass_sliding(r, acc):
            r0 = pl.multiple_of(r * GRID_W, GRID_W)
            acc = (acc + cm_s[pl.ds(pad + r0 + hi * GRID_W, GRID_W), :]
                   - cm_s[pl.ds(pad + r0 - (lo + 1) * GRID_W, GRID_W), :])
            finish_row(r, r0, acc)
            return acc

        if k > POOL_SLIDING_MIN_WINDOW:
            acc0 = window_sum(0, -lo, hi)
            finish_row(0, 0, acc0)
            lax.fori_loop(1, rows, row_pass_sliding, acc0)
        else:
            lax.fori_loop(0, rows, row_pass, 0, unroll=POOL_UNROLL)

        def mm(tb, carry):
            r0 = pl.multiple_of(tb * POOL_MM_ROWS, POOL_MM_ROWS)
            y = _dot(d_s[pl.ds(r0, POOL_MM_ROWS), :], pw_ref[0])
            y = (y * ps_ref[...]) * zp_ref[0, pl.ds(r0, POOL_MM_ROWS), :].astype(F32)
            o_ref[0, pl.ds(r0, POOL_MM_ROWS), :] = y.astype(BF16)
            return carry

        lax.fori_loop(0, ls // POOL_MM_ROWS, mm, 0, unroll=POOL_UNROLL)

    for gidx, k in enumerate(POOL_WINDOWS):
        pl.when(g == gidx)(functools.partial(body, k))


def _pool_call(v, zp, pool_w, pool_scale):
    bn, ls, _ = v.shape
    pg = POOL_GROUP
    tok = pl.BlockSpec((1, ls, pg), lambda b, g: (b, 0, g))
    return pl.pallas_call(
        functools.partial(_pool_kernel, ls=ls),
        grid=(bn, N_POOL_GROUPS),
        in_specs=[
            tok,
            tok,
            pl.BlockSpec((1, pg, pg), lambda b, g: (g, 0, 0)),
            pl.BlockSpec((1, pg), lambda b, g: (0, g)),
        ],
        out_specs=tok,
        out_shape=jax.ShapeDtypeStruct((bn, ls, D_MODEL), BF16),
        scratch_shapes=[
            pltpu.VMEM((ls + 2 * POOL_PAD_ROWS * GRID_W, pg), F32),
            pltpu.VMEM((ls, pg), BF16),
            pltpu.VMEM((POOL_TB, pg), F32),
        ],
        compiler_params=pltpu.CompilerParams(
            dimension_semantics=("parallel", "parallel"), vmem_limit_bytes=VMEM_LIMIT_BYTES),
        name="pool",
    )(v, zp, pool_w, pool_scale)


def _out_kernel(yp_ref, ys_ref, gt_ref, x_ref, gate_ref, wpost_ref, wpp_ref, wps_ref, wo_ref, o_ref):
    a = _dot(yp_ref[0], wpp_ref[...])
    b = _dot(ys_ref[0], wps_ref[...])
    merged = (gt_ref[0, :, 0:D_MODEL].astype(F32) * a
              + gt_ref[0, :, D_MODEL:2 * D_MODEL].astype(F32) * b)
    o = _dot(merged.astype(BF16), wo_ref[...])
    ms = jnp.mean(o * o, axis=-1, keepdims=True)
    on = (o * lax.rsqrt(ms + NORM_EPS)) * wpost_ref[...]
    o_ref[0] = x_ref[0] + gate_ref[0] * on


def _out_call(yp, ys, gates, x, mod3, wpost, wpp, wps, wo):
    bn, ls, _ = x.shape
    tm = TM_OUT
    tok = lambda width: pl.BlockSpec((1, tm, width), lambda b, i: (b, i, 0))
    return pl.pallas_call(
        _out_kernel,
        grid=(bn, ls // tm),
        in_specs=[
            tok(D_MODEL), tok(D_INNER), tok(2 * D_MODEL), tok(D_MODEL),
            pl.BlockSpec((1, 1, D_MODEL), lambda b, i: (b, 0, 2)),
            _resident(wpost.shape), _resident(wpp.shape), _resident(wps.shape), _resident(wo.shape),
        ],
        out_specs=tok(D_MODEL),
        out_shape=jax.ShapeDtypeStruct(x.shape, F32),
        compiler_params=pltpu.CompilerParams(
            dimension_semantics=("parallel", "parallel"), vmem_limit_bytes=VMEM_LIMIT_BYTES),
        name="out",
    )(yp, ys, gates, x, mod3, wpost, wpp, wps, wo)


def _dt_layout_index():
    idx = []
    zero_col = N_DIR * N_HEADS
    for g in range(N_BC_GROUPS):
        blk = [g * HEADS_PER_GROUP + r for r in range(HEADS_PER_GROUP)]
        blk += [N_HEADS + g * HEADS_PER_GROUP + r for r in range(HEADS_PER_GROUP)]
        blk += [zero_col] * (LANES - len(blk))
        idx += blk
    return jnp.asarray(idx, jnp.int32)


def _dt_layout(cols):
    padded = jnp.concatenate([cols, jnp.zeros((cols.shape[0], 1), cols.dtype)], axis=1)
    return jnp.take(padded, _dt_layout_index(), axis=1)


def kernel(x, c, ctx, c_ctx, w_ada, b_ada, norm_pre, norm_post, w_in, b_merge, pool_w, pool_scale,
           conv_w, conv_b, dt_bias, a_log, d_skip, ssd_norm, w_proj_pool, w_proj_ssd, w_out):
    assert w_ada.shape[0] == 1, "single-layer block"
    bn = x.shape[0]
    assert x.shape[1] % (GRID_W * GRID_W) == 0 and x.shape[1] // GRID_W == GRID_W

    n_cond = bn + 1
    cond_rows = -(-n_cond // SUBLANES) * SUBLANES
    cc = jnp.concatenate([c, c_ctx[None, :], jnp.zeros((cond_rows - n_cond, D_MODEL), F32)], axis=0)
    mod = _mod_call(cc, w_ada[0], b_ada[0][None, :])
    mod3 = mod.reshape(cond_rows, 1, 3 * D_MODEL)

    w_in0 = w_in[0]
    w_main = w_in0[:, :OFF_DT].astype(BF16)
    w_dt = _dt_layout(w_in0[:, OFF_DT:]).astype(BF16)
    dtb_lay = _dt_layout(dt_bias[0].reshape(1, N_DIR * N_HEADS))
    alog_lay = _dt_layout(a_log[0].reshape(1, N_DIR * N_HEADS))
    wpre = norm_pre[0][None, :]
    wpost = norm_post[0][None, :]
    cw = conv_w[0]
    cb = conv_b[0][None, :]
    dsk = jnp.repeat(d_skip[0], HEAD_DIM)[None, :]
    nw = ssd_norm[0][None, :]

    xbc_c, dt_c, acs_c = _inproj_call(ctx, mod3, bn, wpre, w_main, w_dt, None, cw, cb,
                                      dtb_lay, alog_lay, ctx=True)
    h0 = _ssd_call(xbc_c, dt_c, acs_c, None, None, None, None)

    v, zp, zs, gates, xbc, dt, acs = _inproj_call(
        x, mod3, None, wpre, w_main, w_dt, b_merge[0][None, :], cw, cb, dtb_lay, alog_lay,
        ctx=False)
    y_pool = _pool_call(v, zp, pool_w[0].astype(BF16), pool_scale[0][None, :])
    y_ssd = _ssd_call(xbc, dt, acs, h0, zs, dsk, nw)
    return _out_call(y_pool, y_ssd, gates, x, mod3, wpost,
                     w_proj_pool[0].astype(BF16), w_proj_ssd[0].astype(BF16), w_out[0].astype(BF16))
```
